```python
import math
import jax, jax.numpy as jnp
from jax import lax
import numpy as np

D_MODEL = 1024
BATCH = 8
SEQ = 8192
DEPTH = 1

D_MIX = D_MODEL
D_RNN = D_MIX // 2
D_ATTN = D_MIX - D_RNN
HEAD_DIM = 64
N_HEADS = D_ATTN // HEAD_DIM
N_RNN_BLOCKS = D_RNN // HEAD_DIM
RNN_BLOCK = D_RNN // N_RNN_BLOCKS
CONV_WIDTH = 4
LRU_C = 8.0
MOBA_BLOCK = 256
MOBA_TOPK = 3
Q_CHUNK = 128
ROPE_THETA = 500000.0
ROPE_DIM = HEAD_DIM // 4
D_FF = 2816
N_IN = 2 * D_RNN + 3 * D_ATTN
N_MOD = 9
ALPHA = (2.0 * DEPTH) ** 0.25
BETA = (8.0 * DEPTH) ** -0.25
LN_EPS = 1e-5
RMS_EPS = 1e-6
NEG_INF = -1e30

kernel_name = 'hymba_rglru_moba_macaron_deepnorm_adaln'


def layer_norm(x, g, b):
    xf = x.astype(jnp.float32)
    mu = jnp.mean(xf, axis=-1, keepdims=True)
    var = jnp.mean(jnp.square(xf - mu), axis=-1, keepdims=True)
    return ((xf - mu) * lax.rsqrt(var + LN_EPS) * g + b).astype(x.dtype)


def rms_norm(x, g):
    xf = x.astype(jnp.float32)
    return (xf * lax.rsqrt(jnp.mean(xf * xf, axis=-1, keepdims=True) + RMS_EPS) * g).astype(x.dtype)


def swiglu(h, w_gate, w_up, w_down):
    return (jax.nn.silu(h @ w_gate) * (h @ w_up)) @ w_down


def partial_rope(t, positions):
    half = ROPE_DIM // 2
    inv_freq = ROPE_THETA ** (-jnp.arange(half, dtype=jnp.float32) / half)
    ang = positions.astype(jnp.float32)[..., None] * inv_freq
    cos = jnp.cos(ang)[:, :, None, :]
    sin = jnp.sin(ang)[:, :, None, :]
    x1 = t[..., :half].astype(jnp.float32)
    x2 = t[..., half:ROPE_DIM].astype(jnp.float32)
    rot = jnp.concatenate([x1 * cos - x2 * sin, x2 * cos + x1 * sin], axis=-1).astype(t.dtype)
    return jnp.concatenate([rot, t[..., ROPE_DIM:]], axis=-1)


def causal_depthwise_conv(u, w, b):
    out = lax.conv_general_dilated(
        u, w[:, None, :], window_strides=(1,), padding=[(CONV_WIDTH - 1, 0)],
        dimension_numbers=('NWC', 'WIO', 'NWC'), feature_group_count=u.shape[-1])
    return out + b


def rg_lru(u, w_a, b_a, w_x, b_x, lam):
    B, S, _ = u.shape
    ub = u.reshape(B, S, N_RNN_BLOCKS, RNN_BLOCK)
    r = jax.nn.sigmoid(jnp.einsum('bsnd,nde->bsne', ub, w_a).reshape(B, S, D_RNN) + b_a)
    i = jax.nn.sigmoid(jnp.einsum('bsnd,nde->bsne', ub, w_x).reshape(B, S, D_RNN) + b_x)
    log_a = -LRU_C * r.astype(jnp.float32) * jax.nn.softplus(-lam.astype(jnp.float32))
    a = jnp.exp(log_a)
    inp = jnp.sqrt(-jnp.expm1(2.0 * log_a)) * (i * u).astype(jnp.float32)

    def combine(left, right):
        a1, b1 = left
        a2, b2 = right
        return a1 * a2, a2 * b1 + b2

    _, h = lax.associative_scan(combine, (a, inp), axis=1)
    return h.astype(u.dtype)


def moba_attention(q, k, v):
    B, S, H, Dh = q.shape
    nb = -(-S // MOBA_BLOCK)
    pad = nb * MOBA_BLOCK - S

    def to_blocks(t):
        t = jnp.pad(t, ((0, 0), (0, pad), (0, 0), (0, 0)))
        return t.reshape(B, nb, MOBA_BLOCK, H, Dh).transpose(0, 3, 1, 2, 4)

    k_blocks = to_blocks(k)
    v_blocks = to_blocks(v)
    k_mean = jnp.mean(k_blocks.astype(jnp.float32), axis=3)
    n_sel = min(MOBA_TOPK, nb)
    nc = S // Q_CHUNK
    scale = 1.0 / math.sqrt(Dh)
    q_chunks = q.reshape(B, nc, Q_CHUNK, H, Dh).transpose(0, 1, 3, 2, 4).reshape(B * nc, H, Q_CHUNK, Dh)
    blk_ids = jnp.arange(nb)
    key_in_blk = jnp.arange(MOBA_BLOCK)
    gather = jax.vmap(lambda t, idx: t[idx])

    def attend_chunk(args):
        q_c, step = args
        b = step // nc
        q_start = (step % nc) * Q_CHUNK
        qblk = q_start // MOBA_BLOCK
        q_in_blk = q_start % MOBA_BLOCK + jnp.arange(Q_CHUNK)
        kb = k_blocks[b]
        vb = v_blocks[b]
        km = k_mean[b]
        gate = jnp.einsum('hqd,hnd->hqn', q_c.astype(jnp.float32), km)
        gate = jnp.where(blk_ids[None, None, :] < qblk, gate, NEG_INF)
        _, idx = lax.top_k(gate, n_sel)
        valid = idx < qblk
        k_sel = gather(kb, idx)
        v_sel = gather(vb, idx)
        s_sel = jnp.einsum('hqd,hqnkd->hqnk', q_c, k_sel).astype(jnp.float32) * scale
        s_sel = jnp.where(valid[..., None], s_sel, NEG_INF).reshape(H, Q_CHUNK, n_sel * MOBA_BLOCK)
        k_own = lax.dynamic_index_in_dim(kb, qblk, axis=1, keepdims=False)
        v_own = lax.dynamic_index_in_dim(vb, qblk, axis=1, keepdims=False)
        s_own = jnp.einsum('hqd,hkd->hqk', q_c, k_own).astype(jnp.float32) * scale
        s_own = jnp.where(key_in_blk[None, None, :] <= q_in_blk[None, :, None], s_own, NEG_INF)
        p = jax.nn.softmax(jnp.concatenate([s_sel, s_own], axis=-1), axis=-1).astype(v.dtype)
        p_sel = p[..., :n_sel * MOBA_BLOCK].reshape(H, Q_CHUNK, n_sel, MOBA_BLOCK)
        p_own = p[..., n_sel * MOBA_BLOCK:]
        return (jnp.einsum('hqnk,hqnkd->hqd', p_sel, v_sel)
                + jnp.einsum('hqk,hkd->hqd', p_own, v_own))

    out = lax.map(attend_chunk, (q_chunks, jnp.arange(B * nc, dtype=jnp.int32)))
    return out.reshape(B, nc, H, Q_CHUNK, Dh).transpose(0, 1, 3, 2, 4).reshape(B, S, H * Dh)


def hybrid_mixer(h, positions, w_in, conv_w, conv_b, lru_wa, lru_ba, lru_wx, lru_bx, lru_lambda,
                 norm_rnn_g, norm_attn_g, w_out):
    B, S, _ = h.shape
    proj = h @ w_in
    u, g, q, k, v = jnp.split(
        proj, [D_RNN, 2 * D_RNN, 2 * D_RNN + D_ATTN, 2 * D_RNN + 2 * D_ATTN], axis=-1)
    u = causal_depthwise_conv(u, conv_w, conv_b)
    y_rnn = rg_lru(u, lru_wa, lru_ba, lru_wx, lru_bx, lru_lambda) * jax.nn.gelu(g)
    q = partial_rope(q.reshape(B, S, N_HEADS, HEAD_DIM), positions)
    k = partial_rope(k.reshape(B, S, N_HEADS, HEAD_DIM), positions)
    v = v.reshape(B, S, N_HEADS, HEAD_DIM)
    y_attn = moba_attention(q, k, v)
    y = jnp.concatenate([rms_norm(y_rnn, norm_rnn_g), rms_norm(y_attn, norm_attn_g)], axis=-1)
    return y @ w_out


def setup_inputs(seed: int = 0) -> dict:
    key = jax.random.key(seed)
    ks = jax.random.split(key, 32)
    f32 = jnp.float32
    L, D = DEPTH, D_MODEL

    def nrm(k, shape, fan_in, scale=1.0):
        return jax.random.normal(k, shape, f32) * (scale * fan_in ** -0.5)

    def gain(k, shape):
        return 1.0 + 0.02 * jax.random.normal(k, shape, f32)

    def small(k, shape):
        return 0.01 * jax.random.normal(k, shape, f32)

    a_c = jax.random.uniform(ks[20], (L, D_RNN), f32, 0.9, 0.999)
    s = a_c ** (1.0 / LRU_C)
    lru_lambda = jnp.log(s) - jnp.log1p(-s)
    return {
        'x': jax.random.normal(ks[0], (BATCH, SEQ, D), f32),
        'c': jax.random.normal(ks[1], (BATCH, D), f32),
        'positions': jnp.broadcast_to(jnp.arange(SEQ, dtype=jnp.int32), (BATCH, SEQ)),
        'ada_w': nrm(ks[2], (L, D, N_MOD * D), D, 0.3),
        'ada_b': small(ks[3], (L, N_MOD * D)),
        'ffn1_w_gate': nrm(ks[4], (L, D, D_FF), D),
        'ffn1_w_up': nrm(ks[5], (L, D, D_FF), D),
        'ffn1_w_down': nrm(ks[6], (L, D_FF, D), D_FF, BETA),
        'ln1_g': gain(ks[7], (L, D)),
        'ln1_b': small(ks[8], (L, D)),
        'w_in': nrm(ks[9], (L, D, N_IN), D),
        'conv_w': nrm(ks[10], (L, CONV_WIDTH, D_RNN), CONV_WIDTH),
        'conv_b': small(ks[11], (L, D_RNN)),
        'lru_wa': nrm(ks[12], (L, N_RNN_BLOCKS, RNN_BLOCK, RNN_BLOCK), RNN_BLOCK),
        'lru_ba': small(ks[13], (L, D_RNN)),
        'lru_wx': nrm(ks[14], (L, N_RNN_BLOCKS, RNN_BLOCK, RNN_BLOCK), RNN_BLOCK),
        'lru_bx': small(ks[15], (L, D_RNN)),
        'lru_lambda': lru_lambda,
        'norm_rnn_g': gain(ks[16], (L, D_RNN)),
        'norm_attn_g': gain(ks[17], (L, D_ATTN)),
        'w_out': nrm(ks[18], (L, D_MIX, D), D_MIX, BETA),
        'ln2_g': gain(ks[19], (L, D)),
        'ln2_b': small(ks[21], (L, D)),
        'ffn2_w_gate': nrm(ks[22], (L, D, D_FF), D),
        'ffn2_w_up': nrm(ks[23], (L, D, D_FF), D),
        'ffn2_w_down': nrm(ks[24], (L, D_FF, D), D_FF, BETA),
        'ln3_g': gain(ks[25], (L, D)),
        'ln3_b': small(ks[26], (L, D)),
    }


def reference(x, c, positions, ada_w, ada_b, ffn1_w_gate, ffn1_w_up, ffn1_w_down, ln1_g, ln1_b,
              w_in, conv_w, conv_b, lru_wa, lru_ba, lru_wx, lru_bx, lru_lambda, norm_rnn_g,
              norm_attn_g, w_out, ln2_g, ln2_b, ffn2_w_gate, ffn2_w_up, ffn2_w_down, ln3_g, ln3_b):
    c_act = jax.nn.silu(c)
    for l in range(DEPTH):
        mod = (c_act @ ada_w[l] + ada_b[l])[:, None, :]
        sh1, sc1, g1, sh2, sc2, g2, sh3, sc3, g3 = jnp.split(mod, N_MOD, axis=-1)
        h = x * (1.0 + sc1) + sh1
        x = layer_norm(ALPHA * x + 0.5 * (1.0 + g1) * swiglu(h, ffn1_w_gate[l], ffn1_w_up[l], ffn1_w_down[l]),
                       ln1_g[l], ln1_b[l])
        h = x * (1.0 + sc2) + sh2
        y = hybrid_mixer(h, positions, w_in[l], conv_w[l], conv_b[l], lru_wa[l], lru_ba[l], lru_wx[l],
                         lru_bx[l], lru_lambda[l], norm_rnn_g[l], norm_attn_g[l], w_out[l])
        x = layer_norm(ALPHA * x + (1.0 + g2) * y, ln2_g[l], ln2_b[l])
        h = x * (1.0 + sc3) + sh3
        x = layer_norm(ALPHA * x + 0.5 * (1.0 + g3) * swiglu(h, ffn2_w_gate[l], ffn2_w_up[l], ffn2_w_down[l]),
                       ln3_g[l], ln3_b[l])
    return x
```

```python
import functools
import math

import jax
import jax.numpy as jnp
from jax import lax
from jax.experimental import pallas as pl
from jax.experimental.pallas import tpu as pltpu

F32 = jnp.float32
BF16 = jnp.bfloat16

D_MODEL = 1024
D_RNN = 512
D_ATTN = 512
HEAD_DIM = 64
N_HEADS = D_ATTN // HEAD_DIM
N_PAIRS = N_HEADS // 2
N_RNN_BLOCKS = D_RNN // HEAD_DIM
CONV_WIDTH = 4
LRU_C = 8.0
MOBA_BLOCK = 256
MOBA_TOPK = 3
ROPE_THETA = 500000.0
ROPE_DIM = HEAD_DIM // 4
ROPE_HALF = ROPE_DIM // 2
D_FF = 2816
FF_CHUNKS = ((0, 1024), (1024, 1024), (2048, 768))
N_MOD = 9
ALPHA = 2.0 ** 0.25
LN_EPS = 1e-5
RMS_EPS = 1e-6
NEG_INF = -1e30

LANES = 128
VMEM_LIMIT_BYTES = 56 * 1024 * 1024

TM_FFN = 512
TM_PROJ = 512
T_SCAN = 256


def _dot(a, b):
    return jnp.dot(a, b, preferred_element_type=F32)


def _dot_nt(a, b):
    return lax.dot_general(a, b, (((1,), (1,)), ((), ())), preferred_element_type=F32)


def _dot_tn(a, b):
    return lax.dot_general(a, b, (((0,), (0,)), ((), ())), preferred_element_type=F32)


def _layer_norm(v, g, b):
    mu = jnp.mean(v, axis=-1, keepdims=True)
    c = v - mu
    var = jnp.mean(c * c, axis=-1, keepdims=True)
    return c * lax.rsqrt(var + LN_EPS) * g + b


def _sigmoid(v):
    return 1.0 / (1.0 + jnp.exp(-v))


def _resident(shape):
    nd = len(shape)
    return pl.BlockSpec(shape, lambda *_: (0,) * nd, pipeline_mode=pl.Buffered(1))


def _cparams(n_grid):
    return pltpu.CompilerParams(dimension_semantics=("arbitrary",) * n_grid,
                                vmem_limit_bytes=VMEM_LIMIT_BYTES)


def _mod_kernel(c_ref, w_ref, b_ref, o_ref):
    c = c_ref[...]
    a = c * _sigmoid(c)
    a_hi = a.astype(BF16)
    a_lo = (a - a_hi.astype(F32)).astype(BF16)
    w = w_ref[...]
    w_hi = w.astype(BF16)
    w_lo = (w - w_hi.astype(F32)).astype(BF16)
    o_ref[...] = _dot(a_hi, w_hi) + _dot(a_hi, w_lo) + _dot(a_lo, w_hi) + b_ref[...]


def _mod_call(c, ada_w, ada_b):
    bsz = c.shape[0]
    n_out = ada_w.shape[1]
    tn = D_MODEL
    return pl.pallas_call(
        _mod_kernel,
        grid=(n_out // tn,),
        in_specs=[pl.BlockSpec((bsz, D_MODEL), lambda j: (0, 0)),
                  pl.BlockSpec((D_MODEL, tn), lambda j: (0, j)),
                  pl.BlockSpec((1, tn), lambda j: (0, j))],
        out_specs=pl.BlockSpec((bsz, tn), lambda j: (0, j)),
        out_shape=jax.ShapeDtypeStruct((bsz, n_out), F32),
        compiler_params=_cparams(1),
        name="mod",
    )(c, ada_w, ada_b.reshape(1, n_out))


def _ffn_sublayer(x, sh, sc, gt, wg_ref, wu_ref, wd_ref, ln_g, ln_b):
    h = (x * (1.0 + sc) + sh).astype(BF16)
    acc = jnp.zeros(x.shape, F32)
    for start, size in FF_CHUNKS:
        a = _dot(h, wg_ref[:, start:start + size])
        up = _dot(h, wu_ref[:, start:start + size])
        act = (a * _sigmoid(a) * up).astype(BF16)
        acc = acc + _dot(act, wd_ref[start:start + size, :])
    return _layer_norm(ALPHA * x + (0.5 * (1.0 + gt)) * acc, ln_g, ln_b)


def _ffn1_kernel(x_ref, mod_ref, wg_ref, wu_ref, wd_ref, lng_ref, lnb_ref, o_ref):
    m = mod_ref[...]
    o_ref[...] = _ffn_sublayer(x_ref[...], m[0:1], m[1:2], m[2:3], wg_ref, wu_ref, wd_ref,
                               lng_ref[...], lnb_ref[...])


def _ffn1_call(x, mod, wg, wu, wd, ln_g, ln_b):
    bsz, seq, d = x.shape
    tok = pl.BlockSpec((None, TM_FFN, d), lambda b, t: (b, t, 0))
    return pl.pallas_call(
        _ffn1_kernel,
        grid=(bsz, seq // TM_FFN),
        in_specs=[tok,
                  pl.BlockSpec((None, N_MOD, d), lambda b, t: (b, 0, 0)),
                  _resident((d, D_FF)), _resident((d, D_FF)), _resident((D_FF, d)),
                  _resident((1, d)), _resident((1, d))],
        out_specs=tok,
        out_shape=jax.ShapeDtypeStruct(x.shape, F32),
        compiler_params=_cparams(2),
        name="ffn1",
    )(x, mod, wg, wu, wd, ln_g, ln_b)


def _ffn2_kernel(x_ref, yr_ref, yat_ref, mod_ref, wor_ref, woa_ref, nag_ref, ln2g_ref, ln2b_ref,
                 wg_ref, wu_ref, wd_ref, ln3g_ref, ln3b_ref, o_ref):
    m = mod_ref[...]
    x1 = x_ref[...]
    parts = []
    for t2 in range(TM_FFN // MOBA_BLOCK):
        yt = yat_ref[:, t2].reshape(D_ATTN, MOBA_BLOCK).astype(F32)
        ms = jnp.mean(yt * yt, axis=0, keepdims=True)
        yn = (yt * lax.rsqrt(ms + RMS_EPS) * nag_ref[...]).astype(BF16)
        parts.append(_dot_tn(yn, woa_ref[...]))
    y = _dot(yr_ref[...], wor_ref[...]) + jnp.concatenate(parts, axis=0)
    x2 = _layer_norm(ALPHA * x1 + (1.0 + m[5:6]) * y, ln2g_ref[...], ln2b_ref[...])
    o_ref[...] = _ffn_sublayer(x2, m[6:7], m[7:8], m[8:9], wg_ref, wu_ref, wd_ref,
                               ln3g_ref[...], ln3b_ref[...])


def _ffn2_call(x1, yr, yat, mod, wo_r, wo_a, na_g, ln2_g, ln2_b, wg, wu, wd, ln3_g, ln3_b):
    bsz, seq, d = x1.shape
    nt = TM_FFN // MOBA_BLOCK
    tok = pl.BlockSpec((None, TM_FFN, d), lambda b, t: (b, t, 0))
    return pl.pallas_call(
        _ffn2_kernel,
        grid=(bsz, seq // TM_FFN),
        in_specs=[tok,
                  pl.BlockSpec((None, TM_FFN, D_RNN), lambda b, t: (b, t, 0)),
                  pl.BlockSpec((None, N_PAIRS, nt, LANES, MOBA_BLOCK), lambda b, t: (b, 0, t, 0, 0)),
                  pl.BlockSpec((None, N_MOD, d), lambda b, t: (b, 0, 0)),
                  _resident((D_RNN, d)), _resident((D_ATTN, d)), _resident((D_ATTN, 1)),
                  _resident((1, d)), _resident((1, d)),
                  _resident((d, D_FF)), _resident((d, D_FF)), _resident((D_FF, d)),
                  _resident((1, d)), _resident((1, d))],
        out_specs=tok,
        out_shape=jax.ShapeDtypeStruct(x1.shape, F32),
        compiler_params=_cparams(2),
        name="ffn2",
    )(x1, yr, yat, mod, wo_r, wo_a, na_g, ln2_g, ln2_b, wg, wu, wd, ln3_g, ln3_b)


def _inproj_kernel(x_ref, mod_ref, posc_ref, posr_ref, invl_ref, invc_ref, wugk_ref, wqvt_ref,
                   u_ref, g_ref, k_ref, km_ref, qt_ref, vt_ref):
    m = mod_ref[...]
    h = (x_ref[...] * (1.0 + m[4:5]) + m[3:4]).astype(BF16)
    ugk = _dot(h, wugk_ref[...])
    u_ref[...] = ugk[:, :D_RNN]
    g_ref[...] = ugk[:, D_RNN:2 * D_RNN]

    ang = posc_ref[...].astype(F32) * invl_ref[...]
    cos_l = jnp.cos(ang)
    sin_l = jnp.sin(ang)
    lane = lax.broadcasted_iota(jnp.int32, (1, LANES), 1) % HEAD_DIM
    s_lo = jnp.where(lane < ROPE_HALF, -sin_l, 0.0)
    s_hi = jnp.where((lane >= ROPE_HALF) & (lane < ROPE_DIM), sin_l, 0.0)
    kcols = []
    for c in range(D_ATTN // LANES):
        kc = ugk[:, 2 * D_RNN + c * LANES: 2 * D_RNN + (c + 1) * LANES]
        kcols.append(kc * cos_l + pltpu.roll(kc, LANES - ROPE_HALF, 1) * s_lo
                     + pltpu.roll(kc, ROPE_HALF, 1) * s_hi)
    k = jnp.concatenate(kcols, axis=1)
    k_ref[...] = k.astype(BF16)
    nblk = TM_PROJ // MOBA_BLOCK
    km_ref[...] = jnp.concatenate(
        [jnp.mean(k[t2 * MOBA_BLOCK:(t2 + 1) * MOBA_BLOCK], axis=0, keepdims=True) for t2 in range(nblk)],
        axis=0)

    qvt = _dot_nt(wqvt_ref[...], h)
    ang_t = invc_ref[...] * posr_ref[...].astype(F32)
    cos_t = jnp.cos(ang_t)
    sin_t = jnp.sin(ang_t)
    scale = 1.0 / math.sqrt(HEAD_DIM)
    rows = []
    for hd in range(N_HEADS):
        base = hd * HEAD_DIM
        x1 = qvt[base:base + ROPE_HALF]
        x2 = qvt[base + ROPE_HALF:base + ROPE_DIM]
        rows += [x1 * cos_t - x2 * sin_t, x2 * cos_t + x1 * sin_t, qvt[base + ROPE_DIM:base + HEAD_DIM]]
    qt = (jnp.concatenate(rows, axis=0) * scale).astype(BF16)
    vt = qvt[D_ATTN:].astype(BF16)
    for t2 in range(nblk):
        sl = slice(t2 * MOBA_BLOCK, (t2 + 1) * MOBA_BLOCK)
        qt_ref[:, t2] = qt[:, sl].reshape(N_PAIRS, LANES, MOBA_BLOCK)
        vt_ref[:, t2] = vt[:, sl].reshape(N_PAIRS, LANES, MOBA_BLOCK)


def _inproj_call(x1, mod, positions, inv_freq, w_ugk, w_qvt):
    bsz, seq, d = x1.shape
    nt = TM_PROJ // MOBA_BLOCK
    nblk = seq // MOBA_BLOCK
    lane = jnp.arange(LANES) % HEAD_DIM
    inv_lane = jnp.where(lane < ROPE_DIM, inv_freq[lane % ROPE_HALF], 0.0).reshape(1, LANES)
    pos_col = positions.reshape(bsz, seq, 1)
    pos_row = positions.reshape(bsz, seq // TM_PROJ, 1, TM_PROJ)
    tok512 = pl.BlockSpec((None, TM_PROJ, D_RNN), lambda b, t: (b, t, 0))
    blk5 = pl.BlockSpec((None, N_PAIRS, nt, LANES, MOBA_BLOCK), lambda b, t: (b, 0, t, 0, 0))
    shape5 = (bsz, N_PAIRS, nblk, LANES, MOBA_BLOCK)
    return pl.pallas_call(
        _inproj_kernel,
        grid=(bsz, seq // TM_PROJ),
        in_specs=[pl.BlockSpec((None, TM_PROJ, d), lambda b, t: (b, t, 0)),
                  pl.BlockSpec((None, N_MOD, d), lambda b, t: (b, 0, 0)),
                  pl.BlockSpec((None, TM_PROJ, 1), lambda b, t: (b, t, 0)),
                  pl.BlockSpec((None, None, 1, TM_PROJ), lambda b, t: (b, t, 0, 0)),
                  _resident((1, LANES)), _resident((ROPE_HALF, 1)),
                  _resident((d, 3 * D_RNN)), _resident((2 * D_ATTN, d))],
        out_specs=[tok512, tok512, tok512,
                   pl.BlockSpec((None, None, nt, D_ATTN), lambda b, t: (b, t, 0, 0)),
                   blk5, blk5],
        out_shape=[jax.ShapeDtypeStruct((bsz, seq, D_RNN), F32),
                   jax.ShapeDtypeStruct((bsz, seq, D_RNN), F32),
                   jax.ShapeDtypeStruct((bsz, seq, D_ATTN), BF16),
                   jax.ShapeDtypeStruct((bsz, seq // TM_PROJ, nt, D_ATTN), F32),
                   jax.ShapeDtypeStruct(shape5, BF16),
                   jax.ShapeDtypeStruct(shape5, BF16)],
        compiler_params=_cparams(2),
        name="inproj",
    )(x1, mod, pos_col, pos_row, inv_lane, inv_freq.reshape(ROPE_HALF, 1), w_ugk, w_qvt)


def _rglru_kernel(u_ref, g_ref, cw_ref, cb_ref, wgate_ref, bgate_ref, lam_ref, ng_ref, o_ref,
                  ext_ref, h_ref):
    t = pl.program_id(1)
    tt = T_SCAN

    @pl.when(t == 0)
    def _():
        ext_ref[0:8, :] = jnp.zeros((8, D_RNN), F32)
        h_ref[...] = jnp.zeros(h_ref.shape, F32)

    u = u_ref[...]
    ext_ref[8:tt + 8, :] = u
    cw = cw_ref[...]
    uc = (ext_ref[5:tt + 5, :] * cw[0:1] + ext_ref[6:tt + 6, :] * cw[1:2]
          + ext_ref[7:tt + 7, :] * cw[2:3] + u * cw[3:4] + cb_ref[...])
    ext_ref[0:8, :] = u[tt - 8:tt]

    gates = _dot(uc.astype(BF16), wgate_ref[...]) + bgate_ref[...]
    r = _sigmoid(gates[:, :D_RNN])
    ig = _sigmoid(gates[:, D_RNN:])
    z = -lam_ref[...]
    softplus = jnp.maximum(z, 0.0) + jnp.log1p(jnp.exp(-jnp.abs(z)))
    log_a = (-LRU_C) * r * softplus
    a = jnp.exp(log_a)
    b = jnp.sqrt(1.0 - a * a) * (ig * uc)

    row = lax.broadcasted_iota(jnp.int32, (tt, 1), 0)
    d = 1
    while d < tt:
        keep = row >= d
        a_prev = jnp.where(keep, pltpu.roll(a, d, 0), 1.0)
        b_prev = jnp.where(keep, pltpu.roll(b, d, 0), 0.0)
        b = a * b_prev + b
        a = a * a_prev
        d *= 2
    h = a * h_ref[0:1, :] + b
    h_ref[...] = jnp.broadcast_to(h[tt - 1:tt], h_ref.shape)

    g = g_ref[...]
    gelu = 0.5 * g * (1.0 + jnp.tanh(math.sqrt(2.0 / math.pi) * (g + 0.044715 * (g * g * g))))
    y = h * gelu
    ms = jnp.mean(y * y, axis=-1, keepdims=True)
    o_ref[...] = (y * lax.rsqrt(ms + RMS_EPS) * ng_ref[...]).astype(o_ref.dtype)


def _rglru_call(u, g, conv_w, conv_b, w_gate, b_gate, lam, norm_g):
    bsz, seq, _ = u.shape
    tok = pl.BlockSpec((None, T_SCAN, D_RNN), lambda b, t: (b, t, 0))
    return pl.pallas_call(
        _rglru_kernel,
        grid=(bsz, seq // T_SCAN),
        in_specs=[tok, tok,
                  _resident((CONV_WIDTH, D_RNN)), _resident((1, D_RNN)),
                  _resident((D_RNN, 2 * D_RNN)), _resident((1, 2 * D_RNN)),
                  _resident((1, D_RNN)), _resident((1, D_RNN))],
        out_specs=tok,
        out_shape=jax.ShapeDtypeStruct((bsz, seq, D_RNN), BF16),
        scratch_shapes=[pltpu.VMEM((T_SCAN + 8, D_RNN), F32), pltpu.VMEM((8, D_RNN), F32)],
        compiler_params=_cparams(2),
        name="rglru",
    )(u, g, conv_w, conv_b, w_gate, b_gate, lam, norm_g)


def _top3_bias(gate, n_past):
    nblk = gate.shape[0]
    jidx = lax.broadcasted_iota(jnp.int32, gate.shape, 0).astype(F32)
    valid = jidx < n_past.astype(F32)
    g = jnp.where(valid, gate, NEG_INF)
    sel = jnp.zeros(gate.shape, jnp.bool_)
    for _ in range(MOBA_TOPK):
        mx = jnp.max(g, axis=0, keepdims=True)
        first = jnp.min(jnp.where(g == mx, jidx, float(nblk)), axis=0, keepdims=True)
        pick = jidx == first
        sel = sel | pick
        g = jnp.where(pick, -jnp.inf, g)
    return jnp.where(sel & valid, 0.0, NEG_INF)


def _moba_kernel(qt_ref, k_ref, vt_ref, km_ref, o_ref, bias_ref):
    i = pl.program_id(2)
    qt = qt_ref[...]
    nblk = km_ref.shape[0]

    km = km_ref[...]
    lane_km = lax.broadcasted_iota(jnp.int32, km.shape, 1)
    km2 = jnp.concatenate([jnp.where(lane_km < HEAD_DIM, km, 0.0),
                           jnp.where(lane_km >= HEAD_DIM, km, 0.0)], axis=0).astype(BF16)
    gate = _dot(km2, qt)
    bias_ref[0:nblk, :] = _top3_bias(gate[0:nblk], i)
    bias_ref[nblk:2 * nblk, :] = _top3_bias(gate[nblk:2 * nblk], i)

    lane_k = lax.broadcasted_iota(jnp.int32, (MOBA_BLOCK, LANES), 1)
    key_pos = lax.broadcasted_iota(jnp.int32, (MOBA_BLOCK, MOBA_BLOCK), 0)
    qry_pos = lax.broadcasted_iota(jnp.int32, (MOBA_BLOCK, MOBA_BLOCK), 1)

    def attend(j, carry, diag):
        k_blk = k_ref[pl.ds(pl.multiple_of(j * MOBA_BLOCK, MOBA_BLOCK), MOBA_BLOCK), :]
        vt_blk = vt_ref[j]
        out = []
        for hd in range(2):
            m_old, l_old, acc = carry[hd]
            head_lanes = (lane_k >= HEAD_DIM) if hd else (lane_k < HEAD_DIM)
            st = _dot(jnp.where(head_lanes, k_blk, jnp.zeros_like(k_blk)), qt)
            if diag:
                st = jnp.where(key_pos <= qry_pos, st, NEG_INF)
            else:
                st = st + bias_ref[pl.ds(hd * nblk + j, 1), :]
            m_new = jnp.maximum(m_old, jnp.max(st, axis=0, keepdims=True))
            alpha = jnp.exp(m_old - m_new)
            p = jnp.exp(st - m_new)
            l_new = alpha * l_old + jnp.sum(p, axis=0, keepdims=True)
            pv = _dot(vt_blk[hd * HEAD_DIM:(hd + 1) * HEAD_DIM, :], p.astype(BF16))
            out.append((m_new, l_new, alpha * acc + pv))
        return tuple(out)

    init = tuple((jnp.full((1, MOBA_BLOCK), NEG_INF, F32), jnp.zeros((1, MOBA_BLOCK), F32),
                  jnp.zeros((HEAD_DIM, MOBA_BLOCK), F32)) for _ in range(2))
    carry = attend(i, init, True)
    carry = lax.fori_loop(0, i, lambda j, c: attend(j, c, False), carry)
    o_ref[...] = jnp.concatenate([carry[0][2] / carry[0][1], carry[1][2] / carry[1][1]],
                                 axis=0).astype(o_ref.dtype)


def _moba_call(qt, k, vt, kmean):
    bsz, npair, nblk, _, _ = qt.shape
    seq = k.shape[1]
    qblk = pl.BlockSpec((None, None, None, LANES, MOBA_BLOCK), lambda b, p, i: (b, p, i, 0, 0))
    return pl.pallas_call(
        _moba_kernel,
        grid=(bsz, npair, nblk),
        in_specs=[qblk,
                  pl.BlockSpec((None, seq, LANES), lambda b, p, i: (b, 0, p)),
                  pl.BlockSpec((None, None, nblk, LANES, MOBA_BLOCK), lambda b, p, i: (b, p, 0, 0, 0)),
                  pl.BlockSpec((None, nblk, LANES), lambda b, p, i: (b, 0, p))],
        out_specs=qblk,
        out_shape=jax.ShapeDtypeStruct(qt.shape, BF16),
        scratch_shapes=[pltpu.VMEM((2 * nblk, MOBA_BLOCK), F32)],
        compiler_params=_cparams(3),
        name="moba",
    )(qt, k, vt, kmean)


def _block_diag(w):
    n, d, e = w.shape
    eye = jnp.eye(n, dtype=w.dtype)
    return (eye[:, None, :, None] * w[:, :, None, :]).reshape(n * d, n * e)


def kernel(x, c, positions, ada_w, ada_b, ffn1_w_gate, ffn1_w_up, ffn1_w_down, ln1_g, ln1_b, w_in, conv_w, conv_b, lru_wa, lru_ba, lru_wx, lru_bx, lru_lambda, norm_rnn_g, norm_attn_g, w_out, ln2_g, ln2_b, ffn2_w_gate, ffn2_w_up, ffn2_w_down, ln3_g, ln3_b):
    bsz, seq, d = x.shape
    depth = ada_w.shape[0]
    inv_freq = ROPE_THETA ** (-jnp.arange(ROPE_HALF, dtype=F32) / ROPE_HALF)
    row = lambda v: v.reshape(1, -1)
    for l in range(depth):
        mod = _mod_call(c, ada_w[l], ada_b[l]).reshape(bsz, N_MOD, d)
        x1 = _ffn1_call(x, mod, ffn1_w_gate[l].astype(BF16), ffn1_w_up[l].astype(BF16),
                        ffn1_w_down[l].astype(BF16), row(ln1_g[l]), row(ln1_b[l]))

        w_ugk = jnp.concatenate([w_in[l][:, :2 * D_RNN], w_in[l][:, 2 * D_RNN + D_ATTN:2 * D_RNN + 2 * D_ATTN]],
                                axis=1).astype(BF16)
        w_qvt = jnp.concatenate([w_in[l][:, 2 * D_RNN:2 * D_RNN + D_ATTN], w_in[l][:, 2 * D_RNN + 2 * D_ATTN:]],
                                axis=1).T.astype(BF16)
        u, g, k, kmean, qt, vt = _inproj_call(x1, mod, positions, inv_freq, w_ugk, w_qvt)

        w_gate = jnp.concatenate([_block_diag(lru_wa[l]), _block_diag(lru_wx[l])], axis=1).astype(BF16)
        b_gate = jnp.concatenate([lru_ba[l], lru_bx[l]]).reshape(1, -1)
        yr = _rglru_call(u, g, conv_w[l], row(conv_b[l]), w_gate, b_gate, row(lru_lambda[l]), row(norm_rnn_g[l]))

        yat = _moba_call(qt, k, vt, kmean.reshape(bsz, seq // MOBA_BLOCK, D_ATTN))

        x = _ffn2_call(x1, yr, yat, mod, w_out[l][:D_RNN].astype(BF16), w_out[l][D_RNN:].astype(BF16),
                       norm_attn_g[l].reshape(D_ATTN, 1), row(ln2_g[l]), row(ln2_b[l]),
                       ffn2_w_gate[l].astype(BF16), ffn2_w_up[l].astype(BF16), ffn2_w_down[l].astype(BF16),
                       row(ln3_g[l]), row(ln3_b[l]))
    return x
```

```python
import functools
import math

import jax
import jax.numpy as jnp
from jax import lax
from jax.experimental import pallas as pl
from jax.experimental.pallas import tpu as pltpu

F32 = jnp.float32
BF16 = jnp.bfloat16

D_MODEL = 1024
D_RNN = 512
D_ATTN = 512
HEAD_DIM = 64
N_HEADS = D_ATTN // HEAD_DIM
N_PAIRS = N_HEADS // 2
N_RNN_BLOCKS = D_RNN // HEAD_DIM
CONV_WIDTH = 4
LRU_C = 8.0
MOBA_BLOCK = 256
MOBA_TOPK = 3
MOBA_CHUNK_LOG2 = 2
MOBA_CHUNK = 1 << MOBA_CHUNK_LOG2
K_AUG = 256
ROPE_THETA = 500000.0
ROPE_DIM = HEAD_DIM // 4
ROPE_HALF = ROPE_DIM // 2
D_FF = 2816
FF_CHUNKS = ((0, 1024), (1024, 1024), (2048, 768))
N_MOD = 9
ALPHA = 2.0 ** 0.25
LN_EPS = 1e-5
RMS_EPS = 1e-6
NEG_INF = -1e30

LANES = 128
VMEM_LIMIT_BYTES = 56 * 1024 * 1024

TM_FFN = 512
TM_PROJ = 512
T_SCAN = 256


def _dot(a, b):
    return jnp.dot(a, b, preferred_element_type=F32)


def _dot_nt(a, b):
    return lax.dot_general(a, b, (((1,), (1,)), ((), ())), preferred_element_type=F32)


def _dot_tn(a, b):
    return lax.dot_general(a, b, (((0,), (0,)), ((), ())), preferred_element_type=F32)


def _layer_norm(v, g, b):
    mu = jnp.mean(v, axis=-1, keepdims=True)
    c = v - mu
    var = jnp.mean(c * c, axis=-1, keepdims=True)
    return c * lax.rsqrt(var + LN_EPS) * g + b


def _sigmoid(v):
    return 1.0 / (1.0 + jnp.exp(-v))


def _resident(shape):
    nd = len(shape)
    return pl.BlockSpec(shape, lambda *_: (0,) * nd, pipeline_mode=pl.Buffered(1))


def _cparams(n_grid):
    return pltpu.CompilerParams(dimension_semantics=("arbitrary",) * n_grid,
                                vmem_limit_bytes=VMEM_LIMIT_BYTES)


def _mod_kernel(c_ref, w_ref, b_ref, o_ref):
    c = c_ref[...]
    a = c * _sigmoid(c)
    a_hi = a.astype(BF16)
    a_lo = (a - a_hi.astype(F32)).astype(BF16)
    w = w_ref[...]
    w_hi = w.astype(BF16)
    w_lo = (w - w_hi.astype(F32)).astype(BF16)
    o_ref[...] = _dot(a_hi, w_hi) + _dot(a_hi, w_lo) + _dot(a_lo, w_hi) + b_ref[...]


def _mod_call(c, ada_w, ada_b):
    bsz = c.shape[0]
    n_out = ada_w.shape[1]
    tn = D_MODEL
    return pl.pallas_call(
        _mod_kernel,
        grid=(n_out // tn,),
        in_specs=[pl.BlockSpec((bsz, D_MODEL), lambda j: (0, 0)),
                  pl.BlockSpec((D_MODEL, tn), lambda j: (0, j)),
                  pl.BlockSpec((1, tn), lambda j: (0, j))],
        out_specs=pl.BlockSpec((bsz, tn), lambda j: (0, j)),
        out_shape=jax.ShapeDtypeStruct((bsz, n_out), F32),
        compiler_params=_cparams(1),
        name="mod",
    )(c, ada_w, ada_b.reshape(1, n_out))


def _ffn_sublayer(x, sh, sc, gt, wg_ref, wu_ref, wd_ref, ln_g, ln_b):
    h = (x * (1.0 + sc) + sh).astype(BF16)
    acc = jnp.zeros(x.shape, F32)
    for start, size in FF_CHUNKS:
        a = _dot(h, wg_ref[:, start:start + size])
        up = _dot(h, wu_ref[:, start:start + size])
        act = (a * _sigmoid(a) * up).astype(BF16)
        acc = acc + _dot(act, wd_ref[start:start + size, :])
    return _layer_norm(ALPHA * x + (0.5 * (1.0 + gt)) * acc, ln_g, ln_b)


def _ffn1_kernel(x_ref, mod_ref, wg_ref, wu_ref, wd_ref, lng_ref, lnb_ref, o_ref):
    m = mod_ref[...]
    o_ref[...] = _ffn_sublayer(x_ref[...], m[0:1], m[1:2], m[2:3], wg_ref, wu_ref, wd_ref,
                               lng_ref[...], lnb_ref[...])


def _ffn1_call(x, mod, wg, wu, wd, ln_g, ln_b):
    bsz, seq, d = x.shape
    tok = pl.BlockSpec((None, TM_FFN, d), lambda b, t: (b, t, 0))
    return pl.pallas_call(
        _ffn1_kernel,
        grid=(bsz, seq // TM_FFN),
        in_specs=[tok,
                  pl.BlockSpec((None, N_MOD, d), lambda b, t: (b, 0, 0)),
                  _resident((d, D_FF)), _resident((d, D_FF)), _resident((D_FF, d)),
                  _resident((1, d)), _resident((1, d))],
        out_specs=tok,
        out_shape=jax.ShapeDtypeStruct(x.shape, F32),
        compiler_params=_cparams(2),
        name="ffn1",
    )(x, mod, wg, wu, wd, ln_g, ln_b)


def _ffn2_kernel(x_ref, yr_ref, yat_ref, mod_ref, wor_ref, woa_ref, nag_ref, ln2g_ref, ln2b_ref,
                 wg_ref, wu_ref, wd_ref, ln3g_ref, ln3b_ref, o_ref):
    m = mod_ref[...]
    x1 = x_ref[...]
    parts = []
    for t2 in range(TM_FFN // MOBA_BLOCK):
        yt = yat_ref[:, t2].reshape(D_ATTN, MOBA_BLOCK).astype(F32)
        ms = jnp.mean(yt * yt, axis=0, keepdims=True)
        yn = (yt * lax.rsqrt(ms + RMS_EPS) * nag_ref[...]).astype(BF16)
        parts.append(_dot_tn(yn, woa_ref[...]))
    y = _dot(yr_ref[...], wor_ref[...]) + jnp.concatenate(parts, axis=0)
    x2 = _layer_norm(ALPHA * x1 + (1.0 + m[5:6]) * y, ln2g_ref[...], ln2b_ref[...])
    o_ref[...] = _ffn_sublayer(x2, m[6:7], m[7:8], m[8:9], wg_ref, wu_ref, wd_ref,
                               ln3g_ref[...], ln3b_ref[...])


def _ffn2_call(x1, yr, yat, mod, wo_r, wo_a, na_g, ln2_g, ln2_b, wg, wu, wd, ln3_g, ln3_b):
    bsz, seq, d = x1.shape
    nt = TM_FFN // MOBA_BLOCK
    tok = pl.BlockSpec((None, TM_FFN, d), lambda b, t: (b, t, 0))
    return pl.pallas_call(
        _ffn2_kernel,
        grid=(bsz, seq // TM_FFN),
        in_specs=[tok,
                  pl.BlockSpec((None, TM_FFN, D_RNN), lambda b, t: (b, t, 0)),
                  pl.BlockSpec((None, N_PAIRS, nt, LANES, MOBA_BLOCK), lambda b, t: (b, 0, t, 0, 0)),
                  pl.BlockSpec((None, N_MOD, d), lambda b, t: (b, 0, 0)),
                  _resident((D_RNN, d)), _resident((D_ATTN, d)), _resident((D_ATTN, 1)),
                  _resident((1, d)), _resident((1, d)),
                  _resident((d, D_FF)), _resident((d, D_FF)), _resident((D_FF, d)),
                  _resident((1, d)), _resident((1, d))],
        out_specs=tok,
        out_shape=jax.ShapeDtypeStruct(x1.shape, F32),
        compiler_params=_cparams(2),
        name="ffn2",
    )(x1, yr, yat, mod, wo_r, wo_a, na_g, ln2_g, ln2_b, wg, wu, wd, ln3_g, ln3_b)


def _inproj_kernel(x_ref, mod_ref, posc_ref, posr_ref, invl_ref, invc_ref, wugk_ref, wqvt_ref,
                   u_ref, g_ref, k_ref, km_ref, qt_ref, vt_ref):
    m = mod_ref[...]
    h = (x_ref[...] * (1.0 + m[4:5]) + m[3:4]).astype(BF16)
    ugk = _dot(h, wugk_ref[...])
    u_ref[...] = ugk[:, :D_RNN]
    g_ref[...] = ugk[:, D_RNN:2 * D_RNN]

    ang = posc_ref[...].astype(F32) * invl_ref[...]
    cos_l = jnp.cos(ang)
    sin_l = jnp.sin(ang)
    lane = lax.broadcasted_iota(jnp.int32, (1, LANES), 1) % HEAD_DIM
    s_lo = jnp.where(lane < ROPE_HALF, -sin_l, 0.0)
    s_hi = jnp.where((lane >= ROPE_HALF) & (lane < ROPE_DIM), sin_l, 0.0)
    kcols = []
    for c in range(D_ATTN // LANES):
        kc = ugk[:, 2 * D_RNN + c * LANES: 2 * D_RNN + (c + 1) * LANES]
        kcols.append(kc * cos_l + pltpu.roll(kc, LANES - ROPE_HALF, 1) * s_lo
                     + pltpu.roll(kc, ROPE_HALF, 1) * s_hi)
    k = jnp.concatenate(kcols, axis=1)
    nblk = TM_PROJ // MOBA_BLOCK
    lane_oh = lax.broadcasted_iota(jnp.int32, (MOBA_BLOCK, LANES), 1)
    for t2 in range(nblk):
        rows_t2 = slice(t2 * MOBA_BLOCK, (t2 + 1) * MOBA_BLOCK)
        onehot = (lane_oh == pl.program_id(1) * nblk + t2).astype(BF16)
        pieces = []
        for c in range(N_PAIRS):
            pieces += [kcols[c][rows_t2].astype(BF16), onehot]
        k_ref[rows_t2, :] = jnp.concatenate(pieces, axis=1)
    km_ref[...] = jnp.concatenate(
        [jnp.mean(k[t2 * MOBA_BLOCK:(t2 + 1) * MOBA_BLOCK], axis=0, keepdims=True) for t2 in range(nblk)],
        axis=0)

    qvt = _dot_nt(wqvt_ref[...], h)
    ang_t = invc_ref[...] * posr_ref[...].astype(F32)
    cos_t = jnp.cos(ang_t)
    sin_t = jnp.sin(ang_t)
    scale = math.log2(math.e) / math.sqrt(HEAD_DIM)
    rows = []
    for hd in range(N_HEADS):
        base = hd * HEAD_DIM
        x1 = qvt[base:base + ROPE_HALF]
        x2 = qvt[base + ROPE_HALF:base + ROPE_DIM]
        rows += [x1 * cos_t - x2 * sin_t, x2 * cos_t + x1 * sin_t, qvt[base + ROPE_DIM:base + HEAD_DIM]]
    qt = (jnp.concatenate(rows, axis=0) * scale).astype(BF16)
    vt = qvt[D_ATTN:].astype(BF16)
    for t2 in range(nblk):
        sl = slice(t2 * MOBA_BLOCK, (t2 + 1) * MOBA_BLOCK)
        qt_ref[:, t2] = qt[:, sl].reshape(N_PAIRS, LANES, MOBA_BLOCK)
        vt_ref[:, t2] = vt[:, sl].reshape(N_PAIRS, LANES, MOBA_BLOCK)


def _inproj_call(x1, mod, positions, inv_freq, w_ugk, w_qvt):
    bsz, seq, d = x1.shape
    nt = TM_PROJ // MOBA_BLOCK
    nblk = seq // MOBA_BLOCK
    lane = jnp.arange(LANES) % HEAD_DIM
    inv_lane = jnp.where(lane < ROPE_DIM, inv_freq[lane % ROPE_HALF], 0.0).reshape(1, LANES)
    pos_col = positions.reshape(bsz, seq, 1)
    pos_row = positions.reshape(bsz, seq // TM_PROJ, 1, TM_PROJ)
    tok512 = pl.BlockSpec((None, TM_PROJ, D_RNN), lambda b, t: (b, t, 0))
    blk5 = pl.BlockSpec((None, N_PAIRS, nt, LANES, MOBA_BLOCK), lambda b, t: (b, 0, t, 0, 0))
    shape5 = (bsz, N_PAIRS, nblk, LANES, MOBA_BLOCK)
    return pl.pallas_call(
        _inproj_kernel,
        grid=(bsz, seq // TM_PROJ),
        in_specs=[pl.BlockSpec((None, TM_PROJ, d), lambda b, t: (b, t, 0)),
                  pl.BlockSpec((None, N_MOD, d), lambda b, t: (b, 0, 0)),
                  pl.BlockSpec((None, TM_PROJ, 1), lambda b, t: (b, t, 0)),
                  pl.BlockSpec((None, None, 1, TM_PROJ), lambda b, t: (b, t, 0, 0)),
                  _resident((1, LANES)), _resident((ROPE_HALF, 1)),
                  _resident((d, 3 * D_RNN)), _resident((2 * D_ATTN, d))],
        out_specs=[tok512, tok512,
                   pl.BlockSpec((None, TM_PROJ, N_PAIRS * K_AUG), lambda b, t: (b, t, 0)),
                   pl.BlockSpec((None, None, nt, D_ATTN), lambda b, t: (b, t, 0, 0)),
                   blk5, blk5],
        out_shape=[jax.ShapeDtypeStruct((bsz, seq, D_RNN), F32),
                   jax.ShapeDtypeStruct((bsz, seq, D_RNN), F32),
                   jax.ShapeDtypeStruct((bsz, seq, N_PAIRS * K_AUG), BF16),
                   jax.ShapeDtypeStruct((bsz, seq // TM_PROJ, nt, D_ATTN), F32),
                   jax.ShapeDtypeStruct(shape5, BF16),
                   jax.ShapeDtypeStruct(shape5, BF16)],
        compiler_params=_cparams(2),
        name="inproj",
    )(x1, mod, pos_col, pos_row, inv_lane, inv_freq.reshape(ROPE_HALF, 1), w_ugk, w_qvt)


def _rglru_kernel(u_ref, g_ref, cw_ref, cb_ref, wgate_ref, bgate_ref, lam_ref, ng_ref, o_ref,
                  ext_ref, h_ref):
    t = pl.program_id(1)
    tt = T_SCAN

    @pl.when(t == 0)
    def _():
        ext_ref[0:8, :] = jnp.zeros((8, D_RNN), F32)
        h_ref[...] = jnp.zeros(h_ref.shape, F32)

    u = u_ref[...]
    ext_ref[8:tt + 8, :] = u
    cw = cw_ref[...]
    uc = (ext_ref[5:tt + 5, :] * cw[0:1] + ext_ref[6:tt + 6, :] * cw[1:2]
          + ext_ref[7:tt + 7, :] * cw[2:3] + u * cw[3:4] + cb_ref[...])
    ext_ref[0:8, :] = u[tt - 8:tt]

    gates = _dot(uc.astype(BF16), wgate_ref[...]) + bgate_ref[...]
    r = _sigmoid(gates[:, :D_RNN])
    ig = _sigmoid(gates[:, D_RNN:])
    z = -lam_ref[...]
    softplus = jnp.maximum(z, 0.0) + jnp.log1p(jnp.exp(-jnp.abs(z)))
    log_a = (-LRU_C) * r * softplus
    a = jnp.exp(log_a)
    b = jnp.sqrt(1.0 - a * a) * (ig * uc)

    row = lax.broadcasted_iota(jnp.int32, (tt, 1), 0)
    d = 1
    while d < tt:
        keep = row >= d
        a_prev = jnp.where(keep, pltpu.roll(a, d, 0), 1.0)
        b_prev = jnp.where(keep, pltpu.roll(b, d, 0), 0.0)
        b = a * b_prev + b
        a = a * a_prev
        d *= 2
    h = a * h_ref[0:1, :] + b
    h_ref[...] = jnp.broadcast_to(h[tt - 1:tt], h_ref.shape)

    g = g_ref[...]
    gelu = 0.5 * g * (1.0 + jnp.tanh(math.sqrt(2.0 / math.pi) * (g + 0.044715 * (g * g * g))))
    y = h * gelu
    ms = jnp.mean(y * y, axis=-1, keepdims=True)
    o_ref[...] = (y * lax.rsqrt(ms + RMS_EPS) * ng_ref[...]).astype(o_ref.dtype)


def _rglru_call(u, g, conv_w, conv_b, w_gate, b_gate, lam, norm_g):
    bsz, seq, _ = u.shape
    tok = pl.BlockSpec((None, T_SCAN, D_RNN), lambda b, t: (b, t, 0))
    return pl.pallas_call(
        _rglru_kernel,
        grid=(bsz, seq // T_SCAN),
        in_specs=[tok, tok,
                  _resident((CONV_WIDTH, D_RNN)), _resident((1, D_RNN)),
                  _resident((D_RNN, 2 * D_RNN)), _resident((1, 2 * D_RNN)),
                  _resident((1, D_RNN)), _resident((1, D_RNN))],
        out_specs=tok,
        out_shape=jax.ShapeDtypeStruct((bsz, seq, D_RNN), BF16),
        scratch_shapes=[pltpu.VMEM((T_SCAN + 8, D_RNN), F32), pltpu.VMEM((8, D_RNN), F32)],
        compiler_params=_cparams(2),
        name="rglru",
    )(u, g, conv_w, conv_b, w_gate, b_gate, lam, norm_g)


def _top3_bias(gate, qblk):
    nblk = gate.shape[0]
    jidx = lax.broadcasted_iota(jnp.int32, gate.shape, 0).astype(F32)
    own = qblk.astype(F32)
    valid = jidx < own
    g = jnp.where(valid, gate, NEG_INF)
    sel = jnp.zeros(gate.shape, jnp.bool_)
    for _ in range(MOBA_TOPK):
        mx = jnp.max(g, axis=0, keepdims=True)
        first = jnp.min(jnp.where(g == mx, jidx, float(nblk)), axis=0, keepdims=True)
        pick = jidx == first
        sel = sel | pick
        g = jnp.where(pick, -jnp.inf, g)
    return jnp.where((sel & valid) | (jidx == own), 0.0, NEG_INF)


def _moba_kernel(qt_ref, k_ref, vt_ref, km_ref, o_ref, s0_ref, s1_ref, m_ref, acc_ref):
    i = pl.program_id(2)
    qt = qt_ref[...]
    nblk = km_ref.shape[0]
    ckeys = MOBA_CHUNK * MOBA_BLOCK
    acc_rows = HEAD_DIM + 16

    km = km_ref[...]
    lane_km = lax.broadcasted_iota(jnp.int32, km.shape, 1)
    km2 = jnp.concatenate([jnp.where(lane_km < HEAD_DIM, km, 0.0),
                           jnp.where(lane_km >= HEAD_DIM, km, 0.0)], axis=0).astype(BF16)
    gate = _dot(km2, qt)
    feat = lax.broadcasted_iota(jnp.int32, qt.shape, 0)
    zero_q = jnp.zeros_like(qt)
    q_aug = []
    for hd in range(2):
        q_head = jnp.where((feat >= HEAD_DIM) if hd else (feat < HEAD_DIM), qt, zero_q)
        bias = _top3_bias(gate[hd * nblk:(hd + 1) * nblk], i).astype(BF16)
        q_aug.append(jnp.concatenate(
            [q_head, bias, jnp.zeros((K_AUG - LANES - nblk, MOBA_BLOCK), BF16)], axis=0))

    key_in_blk = lax.broadcasted_iota(jnp.int32, (MOBA_BLOCK, MOBA_BLOCK), 0)
    qry_in_blk = lax.broadcasted_iota(jnp.int32, (MOBA_BLOCK, MOBA_BLOCK), 1)
    ones_rows = jnp.ones((acc_rows - HEAD_DIM, MOBA_BLOCK), BF16)

    def scores(c, s_dst):
        k_win = k_ref[pl.ds(pl.multiple_of(c * ckeys, ckeys), ckeys), :]
        for hd in range(2):
            s_dst[hd] = _dot(k_win, q_aug[hd])

    def softmax_pv(c, s_src, causal):
        for hd in range(2):
            m_old = m_ref[hd]

            def piece(r):
                s = s_src[hd, r * MOBA_BLOCK:(r + 1) * MOBA_BLOCK, :]
                if causal:
                    blk_delta = c * MOBA_CHUNK + r - i
                    s = jnp.where(key_in_blk + blk_delta * MOBA_BLOCK <= qry_in_blk, s, NEG_INF)
                return s

            mx = piece(0)
            for r in range(1, MOBA_CHUNK):
                mx = jnp.maximum(mx, piece(r))
            m_new = jnp.maximum(m_old, jnp.max(mx, axis=0, keepdims=True))
            acc = jnp.exp2(m_old - m_new) * acc_ref[hd]
            for r in range(MOBA_CHUNK):
                p = jnp.exp2(piece(r) - m_new).astype(BF16)
                vt_aug = jnp.concatenate(
                    [vt_ref[c * MOBA_CHUNK + r, hd * HEAD_DIM:(hd + 1) * HEAD_DIM, :], ones_rows], axis=0)
                acc = acc + _dot(vt_aug, p)
            m_ref[hd] = m_new
            acc_ref[hd] = acc

    def step(c, s_cur, s_nxt):
        scores(c + 1, s_nxt)
        softmax_pv(c, s_cur, False)

    m_ref[...] = jnp.full(m_ref.shape, NEG_INF, F32)
    acc_ref[...] = jnp.zeros(acc_ref.shape, F32)
    last = lax.shift_right_logical(i, MOBA_CHUNK_LOG2)
    scores(0, s0_ref)

    def body(c, _):
        @pl.when((c & 1) == 0)
        def _():
            step(c, s0_ref, s1_ref)

        @pl.when((c & 1) == 1)
        def _():
            step(c, s1_ref, s0_ref)
        return 0

    lax.fori_loop(0, last, body, 0)

    @pl.when((last & 1) == 0)
    def _():
        softmax_pv(last, s0_ref, True)

    @pl.when((last & 1) == 1)
    def _():
        softmax_pv(last, s1_ref, True)

    o_ref[...] = jnp.concatenate(
        [acc_ref[hd, :HEAD_DIM, :] / acc_ref[hd, HEAD_DIM:HEAD_DIM + 1, :] for hd in range(2)],
        axis=0).astype(o_ref.dtype)


def _moba_call(qt, k, vt, kmean):
    bsz, npair, nblk, _, _ = qt.shape
    seq = k.shape[1]
    qblk = pl.BlockSpec((None, None, None, LANES, MOBA_BLOCK), lambda b, p, i: (b, p, i, 0, 0))
    return pl.pallas_call(
        _moba_kernel,
        grid=(bsz, npair, nblk),
        in_specs=[qblk,
                  pl.BlockSpec((None, seq, K_AUG), lambda b, p, i: (b, 0, p)),
                  pl.BlockSpec((None, None, nblk, LANES, MOBA_BLOCK), lambda b, p, i: (b, p, 0, 0, 0)),
                  pl.BlockSpec((None, nblk, LANES), lambda b, p, i: (b, 0, p))],
        out_specs=qblk,
        out_shape=jax.ShapeDtypeStruct(qt.shape, BF16),
        scratch_shapes=[pltpu.VMEM((2, MOBA_CHUNK * MOBA_BLOCK, MOBA_BLOCK), F32),
                        pltpu.VMEM((2, MOBA_CHUNK * MOBA_BLOCK, MOBA_BLOCK), F32),
                        pltpu.VMEM((2, 1, MOBA_BLOCK), F32),
                        pltpu.VMEM((2, HEAD_DIM + 16, MOBA_BLOCK), F32)],
        compiler_params=_cparams(3),
        name="moba",
    )(qt, k, vt, kmean)


def _block_diag(w):
    n, d, e = w.shape
    eye = jnp.eye(n, dtype=w.dtype)
    return (eye[:, None, :, None] * w[:, :, None, :]).reshape(n * d, n * e)


def kernel(x, c, positions, ada_w, ada_b, ffn1_w_gate, ffn1_w_up, ffn1_w_down, ln1_g, ln1_b, w_in, conv_w, conv_b, lru_wa, lru_ba, lru_wx, lru_bx, lru_lambda, norm_rnn_g, norm_attn_g, w_out, ln2_g, ln2_b, ffn2_w_gate, ffn2_w_up, ffn2_w_down, ln3_g, ln3_b):
    bsz, seq, d = x.shape
    depth = ada_w.shape[0]
    inv_freq = ROPE_THETA ** (-jnp.arange(ROPE_HALF, dtype=F32) / ROPE_HALF)
    row = lambda v: v.reshape(1, -1)
    for l in range(depth):
        mod = _mod_call(c, ada_w[l], ada_b[l]).reshape(bsz, N_MOD, d)
        x1 = _ffn1_call(x, mod, ffn1_w_gate[l].astype(BF16), ffn1_w_up[l].astype(BF16),
                        ffn1_w_down[l].astype(BF16), row(ln1_g[l]), row(ln1_b[l]))

        w_ugk = jnp.concatenate([w_in[l][:, :2 * D_RNN], w_in[l][:, 2 * D_RNN + D_ATTN:2 * D_RNN + 2 * D_ATTN]],
                                axis=1).astype(BF16)
        w_qvt = jnp.concatenate([w_in[l][:, 2 * D_RNN:2 * D_RNN + D_ATTN], w_in[l][:, 2 * D_RNN + 2 * D_ATTN:]],
                                axis=1).T.astype(BF16)
        u, g, k, kmean, qt, vt = _inproj_call(x1, mod, positions, inv_freq, w_ugk, w_qvt)

        w_gate = jnp.concatenate([_block_diag(lru_wa[l]), _block_diag(lru_wx[l])], axis=1).astype(BF16)
        b_gate = jnp.concatenate([lru_ba[l], lru_bx[l]]).reshape(1, -1)
        yr = _rglru_call(u, g, conv_w[l], row(conv_b[l]), w_gate, b_gate, row(lru_lambda[l]), row(norm_rnn_g[l]))

        yat = _moba_call(qt, k, vt, kmean.reshape(bsz, seq // MOBA_BLOCK, D_ATTN))

        x = _ffn2_call(x1, yr, yat, mod, w_out[l][:D_RNN].astype(BF16), w_out[l][D_RNN:].astype(BF16),
                       norm_attn_g[l].reshape(D_ATTN, 1), row(ln2_g[l]), row(ln2_b[l]),
                       ffn2_w_gate[l].astype(BF16), ffn2_w_up[l].astype(BF16), ffn2_w_down[l].astype(BF16),
                       row(ln3_g[l]), row(ln3_b[l]))
    return x
```

```python
import functools
import math

import jax
import jax.numpy as jnp
from jax import lax
from jax.experimental import pallas as pl
from jax.experimental.pallas import tpu as pltpu

F32 = jnp.float32
BF16 = jnp.bfloat16

D_MODEL = 1024
D_RNN = 512
D_ATTN = 512
HEAD_DIM = 64
N_HEADS = D_ATTN // HEAD_DIM
N_PAIRS = N_HEADS // 2
N_RNN_BLOCKS = D_RNN // HEAD_DIM
CONV_WIDTH = 4
LRU_C = 8.0
MOBA_BLOCK = 256
MOBA_TOPK = 3
MOBA_CHUNK_LOG2 = 2
MOBA_CHUNK = 1 << MOBA_CHUNK_LOG2
MOBA_PAIRS_PER_STEP = 2
K_AUG = 256
ROPE_THETA = 500000.0
ROPE_DIM = HEAD_DIM // 4
ROPE_HALF = ROPE_DIM // 2
D_FF = 2816
FF_CHUNKS = ((0, 1024), (1024, 1024), (2048, 768))
N_MOD = 9
ALPHA = 2.0 ** 0.25
LN_EPS = 1e-5
RMS_EPS = 1e-6
NEG_INF = -1e30

LANES = 128
VMEM_LIMIT_BYTES = 56 * 1024 * 1024

TM_FFN = 512
TM_PROJ = 512
T_SCAN = 256


def _dot(a, b):
    return jnp.dot(a, b, preferred_element_type=F32)


def _dot_nt(a, b):
    return lax.dot_general(a, b, (((1,), (1,)), ((), ())), preferred_element_type=F32)


def _dot_tn(a, b):
    return lax.dot_general(a, b, (((0,), (0,)), ((), ())), preferred_element_type=F32)


def _layer_norm(v, g, b):
    mu = jnp.mean(v, axis=-1, keepdims=True)
    c = v - mu
    var = jnp.mean(c * c, axis=-1, keepdims=True)
    return c * lax.rsqrt(var + LN_EPS) * g + b


def _sigmoid(v):
    return 0.5 * jnp.tanh(0.5 * v) + 0.5


def _resident(shape):
    nd = len(shape)
    return pl.BlockSpec(shape, lambda *_: (0,) * nd, pipeline_mode=pl.Buffered(1))


def _cparams(n_grid):
    return pltpu.CompilerParams(dimension_semantics=("arbitrary",) * n_grid,
                                vmem_limit_bytes=VMEM_LIMIT_BYTES)


def _mod_kernel(c_ref, w_ref, b_ref, o_ref):
    c = c_ref[...]
    a = c * _sigmoid(c)
    a_hi = a.astype(BF16)
    a_lo = (a - a_hi.astype(F32)).astype(BF16)
    w = w_ref[...]
    w_hi = w.astype(BF16)
    w_lo = (w - w_hi.astype(F32)).astype(BF16)
    o_ref[...] = _dot(a_hi, w_hi) + _dot(a_hi, w_lo) + _dot(a_lo, w_hi) + b_ref[...]


def _mod_call(c, ada_w, ada_b):
    bsz = c.shape[0]
    n_out = ada_w.shape[1]
    tn = D_MODEL
    return pl.pallas_call(
        _mod_kernel,
        grid=(n_out // tn,),
        in_specs=[pl.BlockSpec((bsz, D_MODEL), lambda j: (0, 0)),
                  pl.BlockSpec((D_MODEL, tn), lambda j: (0, j)),
                  pl.BlockSpec((1, tn), lambda j: (0, j))],
        out_specs=pl.BlockSpec((bsz, tn), lambda j: (0, j)),
        out_shape=jax.ShapeDtypeStruct((bsz, n_out), F32),
        compiler_params=_cparams(1),
        name="mod",
    )(c, ada_w, ada_b.reshape(1, n_out))


def _ffn_sublayer(x, sh, sc, gt, wg_ref, wu_ref, wd_ref, ln_g, ln_b):
    h = (x * (1.0 + sc) + sh).astype(BF16)
    acc = jnp.zeros(x.shape, F32)
    for start, size in FF_CHUNKS:
        a = _dot(h, wg_ref[:, start:start + size])
        up = _dot(h, wu_ref[:, start:start + size])
        act = (a * _sigmoid(a) * up).astype(BF16)
        acc = acc + _dot(act, wd_ref[start:start + size, :])
    return _layer_norm(ALPHA * x + (0.5 * (1.0 + gt)) * acc, ln_g, ln_b)


def _ffn1_kernel(x_ref, mod_ref, wg_ref, wu_ref, wd_ref, lng_ref, lnb_ref, o_ref):
    m = mod_ref[...]
    o_ref[...] = _ffn_sublayer(x_ref[...], m[0:1], m[1:2], m[2:3], wg_ref, wu_ref, wd_ref,
                               lng_ref[...], lnb_ref[...])


def _ffn1_call(x, mod, wg, wu, wd, ln_g, ln_b):
    bsz, seq, d = x.shape
    tok = pl.BlockSpec((None, TM_FFN, d), lambda b, t: (b, t, 0))
    return pl.pallas_call(
        _ffn1_kernel,
        grid=(bsz, seq // TM_FFN),
        in_specs=[tok,
                  pl.BlockSpec((None, N_MOD, d), lambda b, t: (b, 0, 0)),
                  _resident((d, D_FF)), _resident((d, D_FF)), _resident((D_FF, d)),
                  _resident((1, d)), _resident((1, d))],
        out_specs=tok,
        out_shape=jax.ShapeDtypeStruct(x.shape, F32),
        compiler_params=_cparams(2),
        name="ffn1",
    )(x, mod, wg, wu, wd, ln_g, ln_b)


def _ffn2_kernel(x_ref, yr_ref, yat_ref, mod_ref, wor_ref, woa_ref, nag_ref, ln2g_ref, ln2b_ref,
                 wg_ref, wu_ref, wd_ref, ln3g_ref, ln3b_ref, o_ref):
    m = mod_ref[...]
    x1 = x_ref[...]
    parts = []
    for t2 in range(TM_FFN // MOBA_BLOCK):
        yt = yat_ref[:, t2].reshape(D_ATTN, MOBA_BLOCK).astype(F32)
        ms = jnp.mean(yt * yt, axis=0, keepdims=True)
        yn = (yt * lax.rsqrt(ms + RMS_EPS) * nag_ref[...]).astype(BF16)
        parts.append(_dot_tn(yn, woa_ref[...]))
    y = _dot(yr_ref[...], wor_ref[...]) + jnp.concatenate(parts, axis=0)
    x2 = _layer_norm(ALPHA * x1 + (1.0 + m[5:6]) * y, ln2g_ref[...], ln2b_ref[...])
    o_ref[...] = _ffn_sublayer(x2, m[6:7], m[7:8], m[8:9], wg_ref, wu_ref, wd_ref,
                               ln3g_ref[...], ln3b_ref[...])


def _ffn2_call(x1, yr, yat, mod, wo_r, wo_a, na_g, ln2_g, ln2_b, wg, wu, wd, ln3_g, ln3_b):
    bsz, seq, d = x1.shape
    nt = TM_FFN // MOBA_BLOCK
    tok = pl.BlockSpec((None, TM_FFN, d), lambda b, t: (b, t, 0))
    return pl.pallas_call(
        _ffn2_kernel,
        grid=(bsz, seq // TM_FFN),
        in_specs=[tok,
                  pl.BlockSpec((None, TM_FFN, D_RNN), lambda b, t: (b, t, 0)),
                  pl.BlockSpec((None, N_PAIRS, nt, LANES, MOBA_BLOCK), lambda b, t: (b, 0, t, 0, 0)),
                  pl.BlockSpec((None, N_MOD, d), lambda b, t: (b, 0, 0)),
                  _resident((D_RNN, d)), _resident((D_ATTN, d)), _resident((D_ATTN, 1)),
                  _resident((1, d)), _resident((1, d)),
                  _resident((d, D_FF)), _resident((d, D_FF)), _resident((D_FF, d)),
                  _resident((1, d)), _resident((1, d))],
        out_specs=tok,
        out_shape=jax.ShapeDtypeStruct(x1.shape, F32),
        compiler_params=_cparams(2),
        name="ffn2",
    )(x1, yr, yat, mod, wo_r, wo_a, na_g, ln2_g, ln2_b, wg, wu, wd, ln3_g, ln3_b)


def _inproj_kernel(x_ref, mod_ref, posc_ref, posr_ref, invl_ref, invc_ref, wugk_ref, wqvt_ref,
                   u_ref, g_ref, k_ref, km_ref, qt_ref, vt_ref):
    m = mod_ref[...]
    h = (x_ref[...] * (1.0 + m[4:5]) + m[3:4]).astype(BF16)
    ugk = _dot(h, wugk_ref[...])
    u_ref[...] = ugk[:, :D_RNN]
    g_ref[...] = ugk[:, D_RNN:2 * D_RNN]

    ang = posc_ref[...].astype(F32) * invl_ref[...]
    cos_l = jnp.cos(ang)
    sin_l = jnp.sin(ang)
    lane = lax.broadcasted_iota(jnp.int32, (1, LANES), 1) % HEAD_DIM
    s_lo = jnp.where(lane < ROPE_HALF, -sin_l, 0.0)
    s_hi = jnp.where((lane >= ROPE_HALF) & (lane < ROPE_DIM), sin_l, 0.0)
    kcols = []
    for c in range(D_ATTN // LANES):
        kc = ugk[:, 2 * D_RNN + c * LANES: 2 * D_RNN + (c + 1) * LANES]
        kcols.append(kc * cos_l + pltpu.roll(kc, LANES - ROPE_HALF, 1) * s_lo
                     + pltpu.roll(kc, ROPE_HALF, 1) * s_hi)
    k = jnp.concatenate(kcols, axis=1)
    nblk = TM_PROJ // MOBA_BLOCK
    lane_oh = lax.broadcasted_iota(jnp.int32, (MOBA_BLOCK, LANES), 1)
    for t2 in range(nblk):
        rows_t2 = slice(t2 * MOBA_BLOCK, (t2 + 1) * MOBA_BLOCK)
        onehot = (lane_oh == pl.program_id(1) * nblk + t2).astype(BF16)
        pieces = []
        for c in range(N_PAIRS):
            pieces += [kcols[c][rows_t2].astype(BF16), onehot]
        k_ref[rows_t2, :] = jnp.concatenate(pieces, axis=1)
    km_ref[...] = jnp.concatenate(
        [jnp.mean(k[t2 * MOBA_BLOCK:(t2 + 1) * MOBA_BLOCK], axis=0, keepdims=True) for t2 in range(nblk)],
        axis=0)

    qvt = _dot_nt(wqvt_ref[...], h)
    ang_t = invc_ref[...] * posr_ref[...].astype(F32)
    cos_t = jnp.cos(ang_t)
    sin_t = jnp.sin(ang_t)
    scale = math.log2(math.e) / math.sqrt(HEAD_DIM)
    rows = []
    for hd in range(N_HEADS):
        base = hd * HEAD_DIM
        x1 = qvt[base:base + ROPE_HALF]
        x2 = qvt[base + ROPE_HALF:base + ROPE_DIM]
        rows += [x1 * cos_t - x2 * sin_t, x2 * cos_t + x1 * sin_t, qvt[base + ROPE_DIM:base + HEAD_DIM]]
    qt = (jnp.concatenate(rows, axis=0) * scale).astype(BF16)
    vt = qvt[D_ATTN:].astype(BF16)
    for t2 in range(nblk):
        sl = slice(t2 * MOBA_BLOCK, (t2 + 1) * MOBA_BLOCK)
        qt_ref[:, t2] = qt[:, sl].reshape(N_PAIRS, LANES, MOBA_BLOCK)
        vt_ref[:, t2] = vt[:, sl].reshape(N_PAIRS, LANES, MOBA_BLOCK)


def _inproj_call(x1, mod, positions, inv_freq, w_ugk, w_qvt):
    bsz, seq, d = x1.shape
    nt = TM_PROJ // MOBA_BLOCK
    nblk = seq // MOBA_BLOCK
    lane = jnp.arange(LANES) % HEAD_DIM
    inv_lane = jnp.where(lane < ROPE_DIM, inv_freq[lane % ROPE_HALF], 0.0).reshape(1, LANES)
    pos_col = positions.reshape(bsz, seq, 1)
    pos_row = positions.reshape(bsz, seq // TM_PROJ, 1, TM_PROJ)
    tok512 = pl.BlockSpec((None, TM_PROJ, D_RNN), lambda b, t: (b, t, 0))
    blk5 = pl.BlockSpec((None, N_PAIRS, nt, LANES, MOBA_BLOCK), lambda b, t: (b, 0, t, 0, 0))
    shape5 = (bsz, N_PAIRS, nblk, LANES, MOBA_BLOCK)
    return pl.pallas_call(
        _inproj_kernel,
        grid=(bsz, seq // TM_PROJ),
        in_specs=[pl.BlockSpec((None, TM_PROJ, d), lambda b, t: (b, t, 0)),
                  pl.BlockSpec((None, N_MOD, d), lambda b, t: (b, 0, 0)),
                  pl.BlockSpec((None, TM_PROJ, 1), lambda b, t: (b, t, 0)),
                  pl.BlockSpec((None, None, 1, TM_PROJ), lambda b, t: (b, t, 0, 0)),
                  _resident((1, LANES)), _resident((ROPE_HALF, 1)),
                  _resident((d, 3 * D_RNN)), _resident((2 * D_ATTN, d))],
        out_specs=[tok512, tok512,
                   pl.BlockSpec((None, TM_PROJ, N_PAIRS * K_AUG), lambda b, t: (b, t, 0)),
                   pl.BlockSpec((None, None, nt, D_ATTN), lambda b, t: (b, t, 0, 0)),
                   blk5, blk5],
        out_shape=[jax.ShapeDtypeStruct((bsz, seq, D_RNN), F32),
                   jax.ShapeDtypeStruct((bsz, seq, D_RNN), F32),
                   jax.ShapeDtypeStruct((bsz, seq, N_PAIRS * K_AUG), BF16),
                   jax.ShapeDtypeStruct((bsz, seq // TM_PROJ, nt, D_ATTN), F32),
                   jax.ShapeDtypeStruct(shape5, BF16),
                   jax.ShapeDtypeStruct(shape5, BF16)],
        compiler_params=_cparams(2),
        name="inproj",
    )(x1, mod, pos_col, pos_row, inv_lane, inv_freq.reshape(ROPE_HALF, 1), w_ugk, w_qvt)


def _rglru_kernel(u_ref, g_ref, cw_ref, cb_ref, wgate_ref, bgate_ref, lam_ref, ng_ref, o_ref,
                  tail_ref, h_ref):
    t = pl.program_id(1)
    tt = T_SCAN
    sub = 8

    @pl.when(t == 0)
    def _():
        tail_ref[...] = jnp.zeros(tail_ref.shape, F32)
        h_ref[...] = jnp.zeros(h_ref.shape, F32)

    u = u_ref[...]
    ext = jnp.concatenate([tail_ref[...], u], axis=0)
    tail_ref[...] = u[tt - sub:tt]
    cw = cw_ref[...]
    uc = u * cw[3:4] + cb_ref[...]
    for back in range(1, CONV_WIDTH):
        uc = uc + pltpu.roll(ext, back, 0)[sub:] * cw[3 - back:4 - back]

    gates = _dot(uc.astype(BF16), wgate_ref[...]) + bgate_ref[...]
    r = _sigmoid(gates[:, :D_RNN])
    ig = _sigmoid(gates[:, D_RNN:])
    z = -lam_ref[...]
    softplus = jnp.maximum(z, 0.0) + jnp.log1p(jnp.exp(-jnp.abs(z)))
    log_a = (-LRU_C) * r * softplus
    a = jnp.exp(log_a)
    one_m_a2 = 1.0 - a * a
    b = (one_m_a2 * lax.rsqrt(jnp.maximum(one_m_a2, 1e-30))) * (ig * uc)

    a = a.reshape(tt // sub, sub, D_RNN)
    b = b.reshape(tt // sub, sub, D_RNN)
    row_in_grp = lax.broadcasted_iota(jnp.int32, (1, sub, 1), 1)
    d = 1
    while d < sub:
        keep = row_in_grp >= d
        a_prev = jnp.where(keep, pltpu.roll(a, d, 1), 1.0)
        b_prev = jnp.where(keep, pltpu.roll(b, d, 1), 0.0)
        b = a * b_prev + b
        a = a * a_prev
        d *= 2
    a = a.reshape(tt, D_RNN)
    b = b.reshape(tt, D_RNN)
    carry = h_ref[0:1, :]
    groups = []
    for grp in range(tt // sub):
        hg = a[grp * sub:(grp + 1) * sub] * carry + b[grp * sub:(grp + 1) * sub]
        groups.append(hg)
        carry = hg[sub - 1:sub]
    h = jnp.concatenate(groups, axis=0)
    h_ref[...] = jnp.broadcast_to(carry, h_ref.shape)

    g = g_ref[...]
    gelu = 0.5 * g * (1.0 + jnp.tanh(math.sqrt(2.0 / math.pi) * (g + 0.044715 * (g * g * g))))
    y = h * gelu
    ms = jnp.mean(y * y, axis=-1, keepdims=True)
    o_ref[...] = (y * lax.rsqrt(ms + RMS_EPS) * ng_ref[...]).astype(o_ref.dtype)


def _rglru_call(u, g, conv_w, conv_b, w_gate, b_gate, lam, norm_g):
    bsz, seq, _ = u.shape
    tok = pl.BlockSpec((None, T_SCAN, D_RNN), lambda b, t: (b, t, 0))
    return pl.pallas_call(
        _rglru_kernel,
        grid=(bsz, seq // T_SCAN),
        in_specs=[tok, tok,
                  _resident((CONV_WIDTH, D_RNN)), _resident((1, D_RNN)),
                  _resident((D_RNN, 2 * D_RNN)), _resident((1, 2 * D_RNN)),
                  _resident((1, D_RNN)), _resident((1, D_RNN))],
        out_specs=tok,
        out_shape=jax.ShapeDtypeStruct((bsz, seq, D_RNN), BF16),
        scratch_shapes=[pltpu.VMEM((8, D_RNN), F32), pltpu.VMEM((8, D_RNN), F32)],
        compiler_params=_cparams(2),
        name="rglru",
    )(u, g, conv_w, conv_b, w_gate, b_gate, lam, norm_g)


def _top3_bias(gate, qblk):
    nblk = gate.shape[0]
    jidx = lax.broadcasted_iota(jnp.int32, gate.shape, 0).astype(F32)
    own = qblk.astype(F32)
    valid = jidx < own
    g = jnp.where(valid, gate, NEG_INF)
    sel = jnp.zeros(gate.shape, jnp.bool_)
    for _ in range(MOBA_TOPK):
        mx = jnp.max(g, axis=0, keepdims=True)
        first = jnp.min(jnp.where(g == mx, jidx, float(nblk)), axis=0, keepdims=True)
        pick = jidx == first
        sel = sel | pick
        g = jnp.where(pick, -jnp.inf, g)
    return jnp.where((sel & valid) | (jidx == own), 0.0, NEG_INF)


def _moba_kernel(qt_ref, k_ref, vt_ref, km_ref, o_ref, s0_ref, s1_ref, mx0_ref, mx1_ref, m_ref, acc_ref):
    i = pl.program_id(2)
    npr = qt_ref.shape[0]
    nblk = km_ref.shape[0]
    ckeys = MOBA_CHUNK * MOBA_BLOCK
    acc_rows = acc_ref.shape[1]
    streams = [(pr, hd) for pr in range(npr) for hd in range(2)]

    q_aug = []
    for pr in range(npr):
        qt = qt_ref[pr]
        km = km_ref[:, pr * LANES:(pr + 1) * LANES]
        lane_km = lax.broadcasted_iota(jnp.int32, km.shape, 1)
        km2 = jnp.concatenate([jnp.where(lane_km < HEAD_DIM, km, 0.0),
                               jnp.where(lane_km >= HEAD_DIM, km, 0.0)], axis=0).astype(BF16)
        gate = _dot(km2, qt)
        feat = lax.broadcasted_iota(jnp.int32, qt.shape, 0)
        for hd in range(2):
            q_head = jnp.where((feat >= HEAD_DIM) if hd else (feat < HEAD_DIM), qt, jnp.zeros_like(qt))
            bias = _top3_bias(gate[hd * nblk:(hd + 1) * nblk], i).astype(BF16)
            q_aug.append(jnp.concatenate(
                [q_head, bias, jnp.zeros((K_AUG - LANES - nblk, MOBA_BLOCK), BF16)], axis=0))

    key_in_blk = lax.broadcasted_iota(jnp.int32, (MOBA_BLOCK, MOBA_BLOCK), 0)
    qry_in_blk = lax.broadcasted_iota(jnp.int32, (MOBA_BLOCK, MOBA_BLOCK), 1)
    ones_rows = jnp.ones((acc_rows - HEAD_DIM, MOBA_BLOCK), BF16)

    def scores(c, dst):
        s_dst, mx_dst = dst
        rows = pl.ds(pl.multiple_of(c * ckeys, ckeys), ckeys)
        for n, (pr, hd) in enumerate(streams):
            s = _dot(k_ref[rows, pr * K_AUG:(pr + 1) * K_AUG], q_aug[n])
            s_dst[n] = s
            mx_dst[n] = jnp.max(s.reshape(ckeys // 8, 8, MOBA_BLOCK), axis=0)

    def softmax_pv(c, src, causal):
        s_src, mx_src = src
        for n, (pr, hd) in enumerate(streams):
            m_old = m_ref[n]

            def piece(r):
                s = s_src[n, r * MOBA_BLOCK:(r + 1) * MOBA_BLOCK, :]
                if causal:
                    blk_delta = c * MOBA_CHUNK + r - i
                    s = jnp.where(key_in_blk + blk_delta * MOBA_BLOCK <= qry_in_blk, s, NEG_INF)
                return s

            if causal:
                mx = piece(0)
                for r in range(1, MOBA_CHUNK):
                    mx = jnp.maximum(mx, piece(r))
            else:
                mx = mx_src[n]
            m_new = jnp.maximum(m_old, jnp.max(mx, axis=0, keepdims=True))
            acc = jnp.exp2(m_old - m_new) * acc_ref[n]
            for r in range(MOBA_CHUNK):
                p = jnp.exp2(piece(r) - m_new).astype(BF16)
                vt_aug = jnp.concatenate(
                    [vt_ref[pr, c * MOBA_CHUNK + r, hd * HEAD_DIM:(hd + 1) * HEAD_DIM, :], ones_rows], axis=0)
                acc = acc + _dot(vt_aug, p)
            m_ref[n] = m_new
            acc_ref[n] = acc

    def step(c, s_cur, s_nxt):
        scores(c + 1, s_nxt)
        softmax_pv(c, s_cur, False)

    m_ref[...] = jnp.full(m_ref.shape, NEG_INF, F32)
    acc_ref[...] = jnp.zeros(acc_ref.shape, F32)
    last = lax.shift_right_logical(i, MOBA_CHUNK_LOG2)
    buf0, buf1 = (s0_ref, mx0_ref), (s1_ref, mx1_ref)
    scores(0, buf0)

    def two_steps(u, _):
        step(2 * u, buf0, buf1)
        step(2 * u + 1, buf1, buf0)
        return 0

    lax.fori_loop(0, lax.shift_right_logical(last, 1), two_steps, 0)

    @pl.when((last & 1) == 1)
    def _():
        step(last - 1, buf0, buf1)
        softmax_pv(last, buf1, True)

    @pl.when((last & 1) == 0)
    def _():
        softmax_pv(last, buf0, True)

    for pr in range(npr):
        o_ref[pr] = jnp.concatenate(
            [acc_ref[2 * pr + hd, :HEAD_DIM, :] / acc_ref[2 * pr + hd, HEAD_DIM:HEAD_DIM + 1, :] for hd in range(2)],
            axis=0).astype(o_ref.dtype)


def _moba_call(qt, k, vt, kmean):
    bsz, npair, nblk, _, _ = qt.shape
    seq = k.shape[1]
    npr = MOBA_PAIRS_PER_STEP
    n_streams = 2 * npr
    qblk = pl.BlockSpec((None, npr, None, LANES, MOBA_BLOCK), lambda b, p, i: (b, p, i, 0, 0))
    return pl.pallas_call(
        _moba_kernel,
        grid=(bsz, npair // npr, nblk),
        in_specs=[qblk,
                  pl.BlockSpec((None, seq, npr * K_AUG), lambda b, p, i: (b, 0, p)),
                  pl.BlockSpec((None, npr, nblk, LANES, MOBA_BLOCK), lambda b, p, i: (b, p, 0, 0, 0)),
                  pl.BlockSpec((None, nblk, npr * LANES), lambda b, p, i: (b, 0, p))],
        out_specs=qblk,
        out_shape=jax.ShapeDtypeStruct(qt.shape, BF16),
        scratch_shapes=[pltpu.VMEM((n_streams, MOBA_CHUNK * MOBA_BLOCK, MOBA_BLOCK), F32),
                        pltpu.VMEM((n_streams, MOBA_CHUNK * MOBA_BLOCK, MOBA_BLOCK), F32),
                        pltpu.VMEM((n_streams, 8, MOBA_BLOCK), F32),
                        pltpu.VMEM((n_streams, 8, MOBA_BLOCK), F32),
                        pltpu.VMEM((n_streams, 1, MOBA_BLOCK), F32),
                        pltpu.VMEM((n_streams, HEAD_DIM + 16, MOBA_BLOCK), F32)],
        compiler_params=_cparams(3),
        name="moba",
    )(qt, k, vt, kmean)


def _block_diag(w):
    n, d, e = w.shape
    eye = jnp.eye(n, dtype=w.dtype)
    return (eye[:, None, :, None] * w[:, :, None, :]).reshape(n * d, n * e)


def kernel(x, c, positions, ada_w, ada_b, ffn1_w_gate, ffn1_w_up, ffn1_w_down, ln1_g, ln1_b, w_in, conv_w, conv_b, lru_wa, lru_ba, lru_wx, lru_bx, lru_lambda, norm_rnn_g, norm_attn_g, w_out, ln2_g, ln2_b, ffn2_w_gate, ffn2_w_up, ffn2_w_down, ln3_g, ln3_b):
    bsz, seq, d = x.shape
    depth = ada_w.shape[0]
    inv_freq = ROPE_THETA ** (-jnp.arange(ROPE_HALF, dtype=F32) / ROPE_HALF)
    row = lambda v: v.reshape(1, -1)
    for l in range(depth):
        mod = _mod_call(c, ada_w[l], ada_b[l]).reshape(bsz, N_MOD, d)
        x1 = _ffn1_call(x, mod, ffn1_w_gate[l].astype(BF16), ffn1_w_up[l].astype(BF16),
                        ffn1_w_down[l].astype(BF16), row(ln1_g[l]), row(ln1_b[l]))

        w_ugk = jnp.concatenate([w_in[l][:, :2 * D_RNN], w_in[l][:, 2 * D_RNN + D_ATTN:2 * D_RNN + 2 * D_ATTN]],
                                axis=1).astype(BF16)
        w_qvt = jnp.concatenate([w_in[l][:, 2 * D_RNN:2 * D_RNN + D_ATTN], w_in[l][:, 2 * D_RNN + 2 * D_ATTN:]],
                                axis=1).T.astype(BF16)
        u, g, k, kmean, qt, vt = _inproj_call(x1, mod, positions, inv_freq, w_ugk, w_qvt)

        w_gate = jnp.concatenate([_block_diag(lru_wa[l]), _block_diag(lru_wx[l])], axis=1).astype(BF16)
        b_gate = jnp.concatenate([lru_ba[l], lru_bx[l]]).reshape(1, -1)
        yr = _rglru_call(u, g, conv_w[l], row(conv_b[l]), w_gate, b_gate, row(lru_lambda[l]), row(norm_rnn_g[l]))

        yat = _moba_call(qt, k, vt, kmean.reshape(bsz, seq // MOBA_BLOCK, D_ATTN))

        x = _ffn2_call(x1, yr, yat, mod, w_out[l][:D_RNN].astype(BF16), w_out[l][D_RNN:].astype(BF16),
                       norm_attn_g[l].reshape(D_ATTN, 1), row(ln2_g[l]), row(ln2_b[l]),
                       ffn2_w_gate[l].astype(BF16), ffn2_w_up[l].astype(BF16), ffn2_w_down[l].astype(BF16),
                       row(ln3_g[l]), row(ln3_b[l]))
    return x
```

```python
import functools
import math

import jax
import jax.numpy as jnp
from jax import lax
from jax.experimental import pallas as pl
from jax.experimental.pallas import tpu as pltpu

F32 = jnp.float32
BF16 = jnp.bfloat16

D_MODEL = 1024
D_RNN = 512
D_ATTN = 512
HEAD_DIM = 64
N_HEADS = D_ATTN // HEAD_DIM
N_PAIRS = N_HEADS // 2
N_RNN_BLOCKS = D_RNN // HEAD_DIM
CONV_WIDTH = 4
LRU_C = 8.0
MOBA_BLOCK = 256
MOBA_TOPK = 3
MOBA_CHUNK_LOG2 = 2
MOBA_CHUNK = 1 << MOBA_CHUNK_LOG2
MOBA_PAIRS_PER_STEP = 2
MOBA_QBLOCKS_PER_STEP = 2
K_AUG = 256
ROPE_THETA = 500000.0
ROPE_DIM = HEAD_DIM // 4
ROPE_HALF = ROPE_DIM // 2
D_FF = 2816
FF_CHUNKS = ((0, 1024), (1024, 1024), (2048, 768))
N_MOD = 9
ALPHA = 2.0 ** 0.25
LN_EPS = 1e-5
RMS_EPS = 1e-6
NEG_INF = -1e30

LANES = 128
VMEM_LIMIT_BYTES = 56 * 1024 * 1024

TM_FFN = 512
TM_PROJ = 512
T_SCAN = 256


def _dot(a, b):
    return jnp.dot(a, b, preferred_element_type=F32)


def _dot_nt(a, b):
    return lax.dot_general(a, b, (((1,), (1,)), ((), ())), preferred_element_type=F32)


def _dot_tn(a, b):
    return lax.dot_general(a, b, (((0,), (0,)), ((), ())), preferred_element_type=F32)


def _layer_norm(v, g, b):
    mu = jnp.mean(v, axis=-1, keepdims=True)
    c = v - mu
    var = jnp.mean(c * c, axis=-1, keepdims=True)
    return c * lax.rsqrt(var + LN_EPS) * g + b


def _sigmoid(v):
    return 0.5 * jnp.tanh(0.5 * v) + 0.5


def _resident(shape):
    nd = len(shape)
    return pl.BlockSpec(shape, lambda *_: (0,) * nd, pipeline_mode=pl.Buffered(1))


def _cparams(n_grid):
    return pltpu.CompilerParams(dimension_semantics=("arbitrary",) * n_grid,
                                vmem_limit_bytes=VMEM_LIMIT_BYTES)


def _mod_kernel(c_ref, w_ref, b_ref, o_ref):
    c = c_ref[...]
    a = c * _sigmoid(c)
    a_hi = a.astype(BF16)
    a_lo = (a - a_hi.astype(F32)).astype(BF16)
    w = w_ref[...]
    w_hi = w.astype(BF16)
    w_lo = (w - w_hi.astype(F32)).astype(BF16)
    o_ref[...] = _dot(a_hi, w_hi) + _dot(a_hi, w_lo) + _dot(a_lo, w_hi) + b_ref[...]


def _mod_call(c, ada_w, ada_b):
    bsz = c.shape[0]
    n_out = ada_w.shape[1]
    tn = D_MODEL
    return pl.pallas_call(
        _mod_kernel,
        grid=(n_out // tn,),
        in_specs=[pl.BlockSpec((bsz, D_MODEL), lambda j: (0, 0)),
                  pl.BlockSpec((D_MODEL, tn), lambda j: (0, j)),
                  pl.BlockSpec((1, tn), lambda j: (0, j))],
        out_specs=pl.BlockSpec((bsz, tn), lambda j: (0, j)),
        out_shape=jax.ShapeDtypeStruct((bsz, n_out), F32),
        compiler_params=_cparams(1),
        name="mod",
    )(c, ada_w, ada_b.reshape(1, n_out))


def _ffn_sublayer(x, sh, sc, gt, wg_ref, wu_ref, wd_ref, ln_g, ln_b):
    h = (x * (1.0 + sc) + sh).astype(BF16)
    acc = jnp.zeros(x.shape, F32)
    for start, size in FF_CHUNKS:
        a = _dot(h, wg_ref[:, start:start + size])
        up = _dot(h, wu_ref[:, start:start + size])
        act = (a * _sigmoid(a) * up).astype(BF16)
        acc = acc + _dot(act, wd_ref[start:start + size, :])
    return _layer_norm(ALPHA * x + (0.5 * (1.0 + gt)) * acc, ln_g, ln_b)


def _ffn1_kernel(x_ref, mod_ref, wg_ref, wu_ref, wd_ref, lng_ref, lnb_ref, o_ref):
    m = mod_ref[...]
    o_ref[...] = _ffn_sublayer(x_ref[...], m[0:1], m[1:2], m[2:3], wg_ref, wu_ref, wd_ref,
                               lng_ref[...], lnb_ref[...])


def _ffn1_call(x, mod, wg, wu, wd, ln_g, ln_b):
    bsz, seq, d = x.shape
    tok = pl.BlockSpec((None, TM_FFN, d), lambda b, t: (b, t, 0))
    return pl.pallas_call(
        _ffn1_kernel,
        grid=(bsz, seq // TM_FFN),
        in_specs=[tok,
                  pl.BlockSpec((None, N_MOD, d), lambda b, t: (b, 0, 0)),
                  _resident((d, D_FF)), _resident((d, D_FF)), _resident((D_FF, d)),
                  _resident((1, d)), _resident((1, d))],
        out_specs=tok,
        out_shape=jax.ShapeDtypeStruct(x.shape, F32),
        compiler_params=_cparams(2),
        name="ffn1",
    )(x, mod, wg, wu, wd, ln_g, ln_b)


def _ffn2_kernel(x_ref, yr_ref, yat_ref, mod_ref, wor_ref, woa_ref, nag_ref, ln2g_ref, ln2b_ref,
                 wg_ref, wu_ref, wd_ref, ln3g_ref, ln3b_ref, o_ref):
    m = mod_ref[...]
    x1 = x_ref[...]
    parts = []
    for t2 in range(TM_FFN // MOBA_BLOCK):
        yt = yat_ref[:, t2].reshape(D_ATTN, MOBA_BLOCK).astype(F32)
        ms = jnp.mean(yt * yt, axis=0, keepdims=True)
        yn = (yt * lax.rsqrt(ms + RMS_EPS) * nag_ref[...]).astype(BF16)
        parts.append(_dot_tn(yn, woa_ref[...]))
    y = _dot(yr_ref[...], wor_ref[...]) + jnp.concatenate(parts, axis=0)
    x2 = _layer_norm(ALPHA * x1 + (1.0 + m[5:6]) * y, ln2g_ref[...], ln2b_ref[...])
    o_ref[...] = _ffn_sublayer(x2, m[6:7], m[7:8], m[8:9], wg_ref, wu_ref, wd_ref,
                               ln3g_ref[...], ln3b_ref[...])


def _ffn2_call(x1, yr, yat, mod, wo_r, wo_a, na_g, ln2_g, ln2_b, wg, wu, wd, ln3_g, ln3_b):
    bsz, seq, d = x1.shape
    nt = TM_FFN // MOBA_BLOCK
    tok = pl.BlockSpec((None, TM_FFN, d), lambda b, t: (b, t, 0))
    return pl.pallas_call(
        _ffn2_kernel,
        grid=(bsz, seq // TM_FFN),
        in_specs=[tok,
                  pl.BlockSpec((None, TM_FFN, D_RNN), lambda b, t: (b, t, 0)),
                  pl.BlockSpec((None, N_PAIRS, nt, LANES, MOBA_BLOCK), lambda b, t: (b, 0, t, 0, 0)),
                  pl.BlockSpec((None, N_MOD, d), lambda b, t: (b, 0, 0)),
                  _resident((D_RNN, d)), _resident((D_ATTN, d)), _resident((D_ATTN, 1)),
                  _resident((1, d)), _resident((1, d)),
                  _resident((d, D_FF)), _resident((d, D_FF)), _resident((D_FF, d)),
                  _resident((1, d)), _resident((1, d))],
        out_specs=tok,
        out_shape=jax.ShapeDtypeStruct(x1.shape, F32),
        compiler_params=_cparams(2),
        name="ffn2",
    )(x1, yr, yat, mod, wo_r, wo_a, na_g, ln2_g, ln2_b, wg, wu, wd, ln3_g, ln3_b)


def _inproj_kernel(x_ref, mod_ref, posr_ref, invc_ref, expand_ref, wugk_ref, wqvt_ref,
                   u_ref, g_ref, k_ref, km_ref, qt_ref, vt_ref):
    m = mod_ref[...]
    h = (x_ref[...] * (1.0 + m[4:5]) + m[3:4]).astype(BF16)
    ugk = _dot(h, wugk_ref[...])
    u_ref[...] = ugk[:, :D_RNN]
    g_ref[...] = ugk[:, D_RNN:2 * D_RNN]

    ang_t = invc_ref[...] * posr_ref[...].astype(F32)
    cos_t = jnp.cos(ang_t)
    sin_t = jnp.sin(ang_t)

    def bf16_terms(v):
        hi = v.astype(BF16).astype(F32)
        mid = (v - hi).astype(BF16).astype(F32)
        return [hi, mid, v - hi - mid]

    terms = bf16_terms(cos_t) + bf16_terms(sin_t) + [jnp.ones_like(cos_t), jnp.zeros_like(cos_t)]
    trig = _dot_tn(jnp.concatenate(terms, axis=0).astype(BF16), expand_ref[...])
    cos_l = trig[:, :LANES]
    sin_l = trig[:, LANES:]

    lane = lax.broadcasted_iota(jnp.int32, (1, LANES), 1) % HEAD_DIM
    s_lo = jnp.where(lane < ROPE_HALF, -sin_l, 0.0)
    s_hi = jnp.where((lane >= ROPE_HALF) & (lane < ROPE_DIM), sin_l, 0.0)
    kcols = []
    for c in range(D_ATTN // LANES):
        kc = ugk[:, 2 * D_RNN + c * LANES: 2 * D_RNN + (c + 1) * LANES]
        kcols.append(kc * cos_l + pltpu.roll(kc, LANES - ROPE_HALF, 1) * s_lo
                     + pltpu.roll(kc, ROPE_HALF, 1) * s_hi)
    k = jnp.concatenate(kcols, axis=1)
    nblk = TM_PROJ // MOBA_BLOCK
    lane_oh = lax.broadcasted_iota(jnp.int32, (MOBA_BLOCK, LANES), 1)
    for t2 in range(nblk):
        rows_t2 = slice(t2 * MOBA_BLOCK, (t2 + 1) * MOBA_BLOCK)
        onehot = (lane_oh == pl.program_id(1) * nblk + t2).astype(BF16)
        pieces = []
        for c in range(N_PAIRS):
            pieces += [kcols[c][rows_t2].astype(BF16), onehot]
        k_ref[rows_t2, :] = jnp.concatenate(pieces, axis=1)
    km_ref[...] = jnp.concatenate(
        [jnp.mean(k[t2 * MOBA_BLOCK:(t2 + 1) * MOBA_BLOCK], axis=0, keepdims=True) for t2 in range(nblk)],
        axis=0)

    qvt = _dot_nt(wqvt_ref[...], h)
    scale = math.log2(math.e) / math.sqrt(HEAD_DIM)
    rows = []
    for hd in range(N_HEADS):
        base = hd * HEAD_DIM
        x1 = qvt[base:base + ROPE_HALF]
        x2 = qvt[base + ROPE_HALF:base + ROPE_DIM]
        rows += [x1 * cos_t - x2 * sin_t, x2 * cos_t + x1 * sin_t, qvt[base + ROPE_DIM:base + HEAD_DIM]]
    qt = (jnp.concatenate(rows, axis=0) * scale).astype(BF16)
    vt = qvt[D_ATTN:].astype(BF16)
    for t2 in range(nblk):
        sl = slice(t2 * MOBA_BLOCK, (t2 + 1) * MOBA_BLOCK)
        qt_ref[:, t2] = qt[:, sl].reshape(N_PAIRS, LANES, MOBA_BLOCK)
        vt_ref[:, t2] = vt[:, sl].reshape(N_PAIRS, LANES, MOBA_BLOCK)


def _inproj_call(x1, mod, positions, inv_freq, w_ugk, w_qvt):
    bsz, seq, d = x1.shape
    nt = TM_PROJ // MOBA_BLOCK
    nblk = seq // MOBA_BLOCK
    lane = jnp.arange(LANES) % HEAD_DIM
    rotary_lane = lane < ROPE_DIM
    route = ((jnp.arange(ROPE_HALF)[:, None] == (lane % ROPE_HALF)[None, :]) & rotary_lane[None, :]).astype(F32)
    none = jnp.zeros_like(route)
    ones_row = jnp.zeros_like(route).at[0].set(jnp.where(rotary_lane, 0.0, 1.0))
    expand = jnp.concatenate(
        [jnp.concatenate([route, none], axis=1)] * 3 + [jnp.concatenate([none, route], axis=1)] * 3
        + [jnp.concatenate([ones_row, none], axis=1), jnp.concatenate([none, none], axis=1)], axis=0).astype(BF16)
    pos_row = positions.reshape(bsz, seq // TM_PROJ, 1, TM_PROJ)
    tok512 = pl.BlockSpec((None, TM_PROJ, D_RNN), lambda b, t: (b, t, 0))
    blk5 = pl.BlockSpec((None, N_PAIRS, nt, LANES, MOBA_BLOCK), lambda b, t: (b, 0, t, 0, 0))
    shape5 = (bsz, N_PAIRS, nblk, LANES, MOBA_BLOCK)
    return pl.pallas_call(
        _inproj_kernel,
        grid=(bsz, seq // TM_PROJ),
        in_specs=[pl.BlockSpec((None, TM_PROJ, d), lambda b, t: (b, t, 0)),
                  pl.BlockSpec((None, N_MOD, d), lambda b, t: (b, 0, 0)),
                  pl.BlockSpec((None, None, 1, TM_PROJ), lambda b, t: (b, t, 0, 0)),
                  _resident((ROPE_HALF, 1)), _resident((8 * ROPE_HALF, 2 * LANES)),
                  _resident((d, 3 * D_RNN)), _resident((2 * D_ATTN, d))],
        out_specs=[tok512, tok512,
                   pl.BlockSpec((None, TM_PROJ, N_PAIRS * K_AUG), lambda b, t: (b, t, 0)),
                   pl.BlockSpec((None, None, nt, D_ATTN), lambda b, t: (b, t, 0, 0)),
                   blk5, blk5],
        out_shape=[jax.ShapeDtypeStruct((bsz, seq, D_RNN), F32),
                   jax.ShapeDtypeStruct((bsz, seq, D_RNN), F32),
                   jax.ShapeDtypeStruct((bsz, seq, N_PAIRS * K_AUG), BF16),
                   jax.ShapeDtypeStruct((bsz, seq // TM_PROJ, nt, D_ATTN), F32),
                   jax.ShapeDtypeStruct(shape5, BF16),
                   jax.ShapeDtypeStruct(shape5, BF16)],
        compiler_params=_cparams(2),
        name="inproj",
    )(x1, mod, pos_row, inv_freq.reshape(ROPE_HALF, 1), expand, w_ugk, w_qvt)


def _rglru_kernel(u_ref, g_ref, cw_ref, cb_ref, wgate_ref, bgate_ref, lam_ref, ng_ref, o_ref,
                  tail_ref, h_ref):
    t = pl.program_id(1)
    tt = T_SCAN
    sub = 8

    @pl.when(t == 0)
    def _():
        tail_ref[...] = jnp.zeros(tail_ref.shape, F32)
        h_ref[...] = jnp.zeros(h_ref.shape, F32)

    u = u_ref[...]
    ext = jnp.concatenate([tail_ref[...], u], axis=0)
    tail_ref[...] = u[tt - sub:tt]
    cw = cw_ref[...]
    uc = u * cw[3:4] + cb_ref[...]
    for back in range(1, CONV_WIDTH):
        uc = uc + pltpu.roll(ext, back, 0)[sub:] * cw[3 - back:4 - back]

    gates = _dot(uc.astype(BF16), wgate_ref[...]) + bgate_ref[...]
    r = _sigmoid(gates[:, :D_RNN])
    ig = _sigmoid(gates[:, D_RNN:])
    z = -lam_ref[...]
    softplus = jnp.maximum(z, 0.0) + jnp.log1p(jnp.exp(-jnp.abs(z)))
    log_a = (-LRU_C) * r * softplus
    a = jnp.exp(log_a)
    one_m_a2 = 1.0 - a * a
    b = (one_m_a2 * lax.rsqrt(jnp.maximum(one_m_a2, 1e-30))) * (ig * uc)

    a = a.reshape(tt // sub, sub, D_RNN)
    b = b.reshape(tt // sub, sub, D_RNN)
    row_in_grp = lax.broadcasted_iota(jnp.int32, (1, sub, 1), 1)
    d = 1
    while d < sub:
        keep = row_in_grp >= d
        a_prev = jnp.where(keep, pltpu.roll(a, d, 1), 1.0)
        b_prev = jnp.where(keep, pltpu.roll(b, d, 1), 0.0)
        b = a * b_prev + b
        a = a * a_prev
        d *= 2
    a = a.reshape(tt, D_RNN)
    b = b.reshape(tt, D_RNN)
    carry = h_ref[0:1, :]
    groups = []
    for grp in range(tt // sub):
        hg = a[grp * sub:(grp + 1) * sub] * carry + b[grp * sub:(grp + 1) * sub]
        groups.append(hg)
        carry = hg[sub - 1:sub]
    h = jnp.concatenate(groups, axis=0)
    h_ref[...] = jnp.broadcast_to(carry, h_ref.shape)

    g = g_ref[...]
    gelu = 0.5 * g * (1.0 + jnp.tanh(math.sqrt(2.0 / math.pi) * (g + 0.044715 * (g * g * g))))
    y = h * gelu
    ms = jnp.mean(y * y, axis=-1, keepdims=True)
    o_ref[...] = (y * lax.rsqrt(ms + RMS_EPS) * ng_ref[...]).astype(o_ref.dtype)


def _rglru_call(u, g, conv_w, conv_b, w_gate, b_gate, lam, norm_g):
    bsz, seq, _ = u.shape
    tok = pl.BlockSpec((None, T_SCAN, D_RNN), lambda b, t: (b, t, 0))
    return pl.pallas_call(
        _rglru_kernel,
        grid=(bsz, seq // T_SCAN),
        in_specs=[tok, tok,
                  _resident((CONV_WIDTH, D_RNN)), _resident((1, D_RNN)),
                  _resident((D_RNN, 2 * D_RNN)), _resident((1, 2 * D_RNN)),
                  _resident((1, D_RNN)), _resident((1, D_RNN))],
        out_specs=tok,
        out_shape=jax.ShapeDtypeStruct((bsz, seq, D_RNN), BF16),
        scratch_shapes=[pltpu.VMEM((8, D_RNN), F32), pltpu.VMEM((8, D_RNN), F32)],
        compiler_params=_cparams(2),
        name="rglru",
    )(u, g, conv_w, conv_b, w_gate, b_gate, lam, norm_g)


def _top3_bias(gate, qblk):
    nblk = gate.shape[0]
    jidx = lax.broadcasted_iota(jnp.int32, gate.shape, 0).astype(F32)
    own = qblk.astype(F32)
    valid = jidx < own
    g = jnp.where(valid, gate, NEG_INF)
    sel = jnp.zeros(gate.shape, jnp.bool_)
    for _ in range(MOBA_TOPK):
        mx = jnp.max(g, axis=0, keepdims=True)
        first = jnp.min(jnp.where(g == mx, jidx, float(nblk)), axis=0, keepdims=True)
        pick = jidx == first
        sel = sel | pick
        g = jnp.where(pick, -jnp.inf, g)
    return jnp.where((sel & valid) | (jidx == own), 0.0, NEG_INF)


def _moba_kernel(qt_ref, k_ref, vt_ref, km_ref, o_ref, s0_ref, s1_ref, mx0_ref, mx1_ref, m_ref, acc_ref):
    npr, ntl = qt_ref.shape[:2]
    i0 = pl.program_id(2) * ntl
    nblk = km_ref.shape[0]
    ckeys = MOBA_CHUNK * MOBA_BLOCK
    acc_rows = acc_ref.shape[1]
    streams = [(pr, tl, hd) for pr in range(npr) for tl in range(ntl) for hd in range(2)]

    q_aug = []
    for pr in range(npr):
        km = km_ref[:, pr * LANES:(pr + 1) * LANES]
        lane_km = lax.broadcasted_iota(jnp.int32, km.shape, 1)
        km2 = jnp.concatenate([jnp.where(lane_km < HEAD_DIM, km, 0.0),
                               jnp.where(lane_km >= HEAD_DIM, km, 0.0)], axis=0).astype(BF16)
        for tl in range(ntl):
            qt = qt_ref[pr, tl]
            gate = _dot(km2, qt)
            feat = lax.broadcasted_iota(jnp.int32, qt.shape, 0)
            for hd in range(2):
                q_head = jnp.where((feat >= HEAD_DIM) if hd else (feat < HEAD_DIM), qt, jnp.zeros_like(qt))
                bias = _top3_bias(gate[hd * nblk:(hd + 1) * nblk], i0 + tl).astype(BF16)
                q_aug.append(jnp.concatenate(
                    [q_head, bias, jnp.zeros((K_AUG - LANES - nblk, MOBA_BLOCK), BF16)], axis=0))

    key_in_blk = lax.broadcasted_iota(jnp.int32, (MOBA_BLOCK, MOBA_BLOCK), 0)
    qry_in_blk = lax.broadcasted_iota(jnp.int32, (MOBA_BLOCK, MOBA_BLOCK), 1)
    ones_rows = jnp.ones((acc_rows - HEAD_DIM, MOBA_BLOCK), BF16)

    def scores(c, dst, group):
        s_dst, mx_dst = dst
        rows = pl.ds(pl.multiple_of(c * ckeys, ckeys), ckeys)
        for n in group:
            pr = streams[n][0]
            s = _dot(k_ref[rows, pr * K_AUG:(pr + 1) * K_AUG], q_aug[n])
            s_dst[n] = s
            mx_dst[n] = jnp.max(s.reshape(ckeys // 8, 8, MOBA_BLOCK), axis=0)

    def softmax_pv(c, src, causal, group):
        s_src, mx_src = src
        for n in group:
            pr, tl, hd = streams[n]
            m_old = m_ref[n]

            def piece(r):
                s = s_src[n, r * MOBA_BLOCK:(r + 1) * MOBA_BLOCK, :]
                if causal:
                    blk_delta = c * MOBA_CHUNK + r - (i0 + tl)
                    s = jnp.where(key_in_blk + blk_delta * MOBA_BLOCK <= qry_in_blk, s, NEG_INF)
                return s

            if causal:
                mx = piece(0)
                for r in range(1, MOBA_CHUNK):
                    mx = jnp.maximum(mx, piece(r))
            else:
                mx = mx_src[n]
            m_new = jnp.maximum(m_old, jnp.max(mx, axis=0, keepdims=True))
            acc = jnp.exp2(m_old - m_new) * acc_ref[n]
            for r in range(MOBA_CHUNK):
                p = jnp.exp2(piece(r) - m_new).astype(BF16)
                vt_aug = jnp.concatenate(
                    [vt_ref[pr, c * MOBA_CHUNK + r, hd * HEAD_DIM:(hd + 1) * HEAD_DIM, :], ones_rows], axis=0)
                acc = acc + _dot(vt_aug, p)
            m_ref[n] = m_new
            acc_ref[n] = acc

    def step(c, s_cur, s_nxt, group):
        scores(c + 1, s_nxt, group)
        softmax_pv(c, s_cur, False, group)

    m_ref[...] = jnp.full(m_ref.shape, NEG_INF, F32)
    acc_ref[...] = jnp.zeros(acc_ref.shape, F32)
    last = lax.shift_right_logical(i0, MOBA_CHUNK_LOG2)
    buf0, buf1 = (s0_ref, mx0_ref), (s1_ref, mx1_ref)
    group = list(range(len(streams)))
    scores(0, buf0, group)

    def two_steps(u, _):
        step(2 * u, buf0, buf1, group)
        step(2 * u + 1, buf1, buf0, group)
        return 0

    lax.fori_loop(0, lax.shift_right_logical(last, 1), two_steps, 0)

    @pl.when((last & 1) == 1)
    def _():
        step(last - 1, buf0, buf1, group)
        softmax_pv(last, buf1, True, group)

    @pl.when((last & 1) == 0)
    def _():
        softmax_pv(last, buf0, True, group)

    for n in range(0, len(streams), 2):
        pr, tl, _ = streams[n]
        o_ref[pr, tl] = jnp.concatenate(
            [acc_ref[n + hd, :HEAD_DIM, :] / acc_ref[n + hd, HEAD_DIM:HEAD_DIM + 1, :] for hd in range(2)],
            axis=0).astype(o_ref.dtype)


def _moba_call(qt, k, vt, kmean):
    bsz, npair, nblk, _, _ = qt.shape
    seq = k.shape[1]
    npr, ntl = MOBA_PAIRS_PER_STEP, MOBA_QBLOCKS_PER_STEP
    n_streams = 2 * npr * ntl
    qblk = pl.BlockSpec((None, npr, ntl, LANES, MOBA_BLOCK), lambda b, p, i: (b, p, i, 0, 0))
    return pl.pallas_call(
        _moba_kernel,
        grid=(bsz, npair // npr, nblk // ntl),
        in_specs=[qblk,
                  pl.BlockSpec((None, seq, npr * K_AUG), lambda b, p, i: (b, 0, p)),
                  pl.BlockSpec((None, npr, nblk, LANES, MOBA_BLOCK), lambda b, p, i: (b, p, 0, 0, 0)),
                  pl.BlockSpec((None, nblk, npr * LANES), lambda b, p, i: (b, 0, p))],
        out_specs=qblk,
        out_shape=jax.ShapeDtypeStruct(qt.shape, BF16),
        scratch_shapes=[pltpu.VMEM((n_streams, MOBA_CHUNK * MOBA_BLOCK, MOBA_BLOCK), F32),
                        pltpu.VMEM((n_streams, MOBA_CHUNK * MOBA_BLOCK, MOBA_BLOCK), F32),
                        pltpu.VMEM((n_streams, 8, MOBA_BLOCK), F32),
                        pltpu.VMEM((n_streams, 8, MOBA_BLOCK), F32),
                        pltpu.VMEM((n_streams, 1, MOBA_BLOCK), F32),
                        pltpu.VMEM((n_streams, HEAD_DIM + 16, MOBA_BLOCK), F32)],
        compiler_params=_cparams(3),
        name="moba",
    )(qt, k, vt, kmean)


def _block_diag(w):
    n, d, e = w.shape
    eye = jnp.eye(n, dtype=w.dtype)
    return (eye[:, None, :, None] * w[:, :, None, :]).reshape(n * d, n * e)


def kernel(x, c, positions, ada_w, ada_b, ffn1_w_gate, ffn1_w_up, ffn1_w_down, ln1_g, ln1_b, w_in, conv_w, conv_b, lru_wa, lru_ba, lru_wx, lru_bx, lru_lambda, norm_rnn_g, norm_attn_g, w_out, ln2_g, ln2_b, ffn2_w_gate, ffn2_w_up, ffn2_w_down, ln3_g, ln3_b):
    bsz, seq, d = x.shape
    depth = ada_w.shape[0]
    inv_freq = ROPE_THETA ** (-jnp.arange(ROPE_HALF, dtype=F32) / ROPE_HALF)
    row = lambda v: v.reshape(1, -1)
    for l in range(depth):
        mod = _mod_call(c, ada_w[l], ada_b[l]).reshape(bsz, N_MOD, d)
        x1 = _ffn1_call(x, mod, ffn1_w_gate[l].astype(BF16), ffn1_w_up[l].astype(BF16),
                        ffn1_w_down[l].astype(BF16), row(ln1_g[l]), row(ln1_b[l]))

        w_ugk = jnp.concatenate([w_in[l][:, :2 * D_RNN], w_in[l][:, 2 * D_RNN + D_ATTN:2 * D_RNN + 2 * D_ATTN]],
                                axis=1).astype(BF16)
        w_qvt = jnp.concatenate([w_in[l][:, 2 * D_RNN:2 * D_RNN + D_ATTN], w_in[l][:, 2 * D_RNN + 2 * D_ATTN:]],
                                axis=1).T.astype(BF16)
        u, g, k, kmean, qt, vt = _inproj_call(x1, mod, positions, inv_freq, w_ugk, w_qvt)

        w_gate = jnp.concatenate([_block_diag(lru_wa[l]), _block_diag(lru_wx[l])], axis=1).astype(BF16)
        b_gate = jnp.concatenate([lru_ba[l], lru_bx[l]]).reshape(1, -1)
        yr = _rglru_call(u, g, conv_w[l], row(conv_b[l]), w_gate, b_gate, row(lru_lambda[l]), row(norm_rnn_g[l]))

        yat = _moba_call(qt, k, vt, kmean.reshape(bsz, seq // MOBA_BLOCK, D_ATTN))

        x = _ffn2_call(x1, yr, yat, mod, w_out[l][:D_RNN].astype(BF16), w_out[l][D_RNN:].astype(BF16),
                       norm_attn_g[l].reshape(D_ATTN, 1), row(ln2_g[l]), row(ln2_b[l]),
                       ffn2_w_gate[l].astype(BF16), ffn2_w_up[l].astype(BF16), ffn2_w_down[l].astype(BF16),
                       row(ln3_g[l]), row(ln3_b[l]))
    return x
```

```python
import functools
import math

import jax
import jax.numpy as jnp
from jax import lax
from jax.experimental import pallas as pl
from jax.experimental.pallas import tpu as pltpu

F32 = jnp.float32
BF16 = jnp.bfloat16

D_MODEL = 1024
D_RNN = 512
D_ATTN = 512
HEAD_DIM = 64
N_HEADS = D_ATTN // HEAD_DIM
N_PAIRS = N_HEADS // 2
N_RNN_BLOCKS = D_RNN // HEAD_DIM
CONV_WIDTH = 4
LRU_C = 8.0
MOBA_BLOCK = 256
MOBA_TOPK = 3
MOBA_CHUNK_LOG2 = 2
MOBA_CHUNK = 1 << MOBA_CHUNK_LOG2
MOBA_PAIRS_PER_STEP = 2
MOBA_QBLOCKS_PER_STEP = 2
K_AUG = 256
ROPE_THETA = 500000.0
ROPE_DIM = HEAD_DIM // 4
ROPE_HALF = ROPE_DIM // 2
D_FF = 2816
FF_CHUNKS = ((0, 1024), (1024, 1024), (2048, 768))
N_MOD = 9
ALPHA = 2.0 ** 0.25
LN_EPS = 1e-5
RMS_EPS = 1e-6
NEG_INF = -1e30

LANES = 128
VMEM_LIMIT_BYTES = 56 * 1024 * 1024

TM_FFN = 512
TM_PROJ = 512


def _dot(a, b):
    return jnp.dot(a, b, preferred_element_type=F32)


def _dot_nt(a, b):
    return lax.dot_general(a, b, (((1,), (1,)), ((), ())), preferred_element_type=F32)


def _dot_tn(a, b):
    return lax.dot_general(a, b, (((0,), (0,)), ((), ())), preferred_element_type=F32)


def _layer_norm(v, g, b):
    mu = jnp.mean(v, axis=-1, keepdims=True)
    c = v - mu
    var = jnp.mean(c * c, axis=-1, keepdims=True)
    return c * lax.rsqrt(var + LN_EPS) * g + b


def _sigmoid(v):
    return 0.5 * jnp.tanh(0.5 * v) + 0.5


def _resident(shape):
    nd = len(shape)
    return pl.BlockSpec(shape, lambda *_: (0,) * nd, pipeline_mode=pl.Buffered(1))


def _cparams(n_grid):
    return pltpu.CompilerParams(dimension_semantics=("arbitrary",) * n_grid,
                                vmem_limit_bytes=VMEM_LIMIT_BYTES)


def _mod_kernel(c_ref, w_ref, b_ref, o_ref):
    c = c_ref[...]
    a = c * _sigmoid(c)
    a_hi = a.astype(BF16)
    a_lo = (a - a_hi.astype(F32)).astype(BF16)
    w = w_ref[...]
    w_hi = w.astype(BF16)
    w_lo = (w - w_hi.astype(F32)).astype(BF16)
    o_ref[...] = _dot(a_hi, w_hi) + _dot(a_hi, w_lo) + _dot(a_lo, w_hi) + b_ref[...]


def _mod_call(c, ada_w, ada_b):
    bsz = c.shape[0]
    n_out = ada_w.shape[1]
    tn = D_MODEL
    return pl.pallas_call(
        _mod_kernel,
        grid=(n_out // tn,),
        in_specs=[pl.BlockSpec((bsz, D_MODEL), lambda j: (0, 0)),
                  pl.BlockSpec((D_MODEL, tn), lambda j: (0, j)),
                  pl.BlockSpec((1, tn), lambda j: (0, j))],
        out_specs=pl.BlockSpec((bsz, tn), lambda j: (0, j)),
        out_shape=jax.ShapeDtypeStruct((bsz, n_out), F32),
        compiler_params=_cparams(1),
        name="mod",
    )(c, ada_w, ada_b.reshape(1, n_out))


def _ffn_sublayer(x, sh, sc, gt, wg_ref, wu_ref, wd_ref, ln_g, ln_b):
    h = (x * (1.0 + sc) + sh).astype(BF16)
    acc = jnp.zeros(x.shape, F32)
    for start, size in FF_CHUNKS:
        a = _dot(h, wg_ref[:, start:start + size])
        up = _dot(h, wu_ref[:, start:start + size])
        act = (a * _sigmoid(a) * up).astype(BF16)
        acc = acc + _dot(act, wd_ref[start:start + size, :])
    return _layer_norm(ALPHA * x + (0.5 * (1.0 + gt)) * acc, ln_g, ln_b)


def _ffn1_kernel(x_ref, mod_ref, wg_ref, wu_ref, wd_ref, lng_ref, lnb_ref, o_ref):
    m = mod_ref[...]
    o_ref[...] = _ffn_sublayer(x_ref[...], m[0:1], m[1:2], m[2:3], wg_ref, wu_ref, wd_ref,
                               lng_ref[...], lnb_ref[...])


def _ffn1_call(x, mod, wg, wu, wd, ln_g, ln_b):
    bsz, seq, d = x.shape
    tok = pl.BlockSpec((None, TM_FFN, d), lambda b, t: (b, t, 0))
    return pl.pallas_call(
        _ffn1_kernel,
        grid=(bsz, seq // TM_FFN),
        in_specs=[tok,
                  pl.BlockSpec((None, N_MOD, d), lambda b, t: (b, 0, 0)),
                  _resident((d, D_FF)), _resident((d, D_FF)), _resident((D_FF, d)),
                  _resident((1, d)), _resident((1, d))],
        out_specs=tok,
        out_shape=jax.ShapeDtypeStruct(x.shape, F32),
        compiler_params=_cparams(2),
        name="ffn1",
    )(x, mod, wg, wu, wd, ln_g, ln_b)


def _ffn2_kernel(x_ref, yr_ref, yat_ref, mod_ref, wor_ref, woa_ref, nag_ref, ln2g_ref, ln2b_ref,
                 wg_ref, wu_ref, wd_ref, ln3g_ref, ln3b_ref, o_ref):
    m = mod_ref[...]
    x1 = x_ref[...]
    parts = []
    for t2 in range(TM_FFN // MOBA_BLOCK):
        yt = yat_ref[:, t2].reshape(D_ATTN, MOBA_BLOCK).astype(F32)
        ms = jnp.mean(yt * yt, axis=0, keepdims=True)
        yn = (yt * lax.rsqrt(ms + RMS_EPS) * nag_ref[...]).astype(BF16)
        parts.append(_dot_tn(yn, woa_ref[...]))
    y = _dot(yr_ref[...], wor_ref[...]) + jnp.concatenate(parts, axis=0)
    x2 = _layer_norm(ALPHA * x1 + (1.0 + m[5:6]) * y, ln2g_ref[...], ln2b_ref[...])
    o_ref[...] = _ffn_sublayer(x2, m[6:7], m[7:8], m[8:9], wg_ref, wu_ref, wd_ref,
                               ln3g_ref[...], ln3b_ref[...])


def _ffn2_call(x1, yr, yat, mod, wo_r, wo_a, na_g, ln2_g, ln2_b, wg, wu, wd, ln3_g, ln3_b):
    bsz, seq, d = x1.shape
    nt = TM_FFN // MOBA_BLOCK
    tok = pl.BlockSpec((None, TM_FFN, d), lambda b, t: (b, t, 0))
    return pl.pallas_call(
        _ffn2_kernel,
        grid=(bsz, seq // TM_FFN),
        in_specs=[tok,
                  pl.BlockSpec((None, TM_FFN, D_RNN), lambda b, t: (b, t, 0)),
                  pl.BlockSpec((None, N_PAIRS, nt, LANES, MOBA_BLOCK), lambda b, t: (b, 0, t, 0, 0)),
                  pl.BlockSpec((None, N_MOD, d), lambda b, t: (b, 0, 0)),
                  _resident((D_RNN, d)), _resident((D_ATTN, d)), _resident((D_ATTN, 1)),
                  _resident((1, d)), _resident((1, d)),
                  _resident((d, D_FF)), _resident((d, D_FF)), _resident((D_FF, d)),
                  _resident((1, d)), _resident((1, d))],
        out_specs=tok,
        out_shape=jax.ShapeDtypeStruct(x1.shape, F32),
        compiler_params=_cparams(2),
        name="ffn2",
    )(x1, yr, yat, mod, wo_r, wo_a, na_g, ln2_g, ln2_b, wg, wu, wd, ln3_g, ln3_b)


def _inproj_kernel(x_ref, mod_ref, posr_ref, invc_ref, expand_ref, wugk_ref, wqvt_ref,
                   cw_ref, cb_ref, wgate_ref, bgate_ref, lam_ref, ng_ref,
                   yr_ref, k_ref, km_ref, qt_ref, vt_ref, tail_ref, hstate_ref):
    @pl.when(pl.program_id(1) == 0)
    def _():
        tail_ref[...] = jnp.zeros(tail_ref.shape, F32)
        hstate_ref[...] = jnp.zeros(hstate_ref.shape, F32)

    m = mod_ref[...]
    h = (x_ref[...] * (1.0 + m[4:5]) + m[3:4]).astype(BF16)
    ugk = _dot(h, wugk_ref[...])
    yr_ref[...] = _rglru_tile(ugk[:, :D_RNN], ugk[:, D_RNN:2 * D_RNN], cw_ref, cb_ref, wgate_ref, bgate_ref,
                              lam_ref, ng_ref, tail_ref, hstate_ref).astype(yr_ref.dtype)

    ang_t = invc_ref[...] * posr_ref[...].astype(F32)
    cos_t = jnp.cos(ang_t)
    sin_t = jnp.sin(ang_t)

    def bf16_terms(v):
        hi = v.astype(BF16).astype(F32)
        mid = (v - hi).astype(BF16).astype(F32)
        return [hi, mid, v - hi - mid]

    terms = bf16_terms(cos_t) + bf16_terms(sin_t) + [jnp.ones_like(cos_t), jnp.zeros_like(cos_t)]
    trig = _dot_tn(jnp.concatenate(terms, axis=0).astype(BF16), expand_ref[...])
    cos_l = trig[:, :LANES]
    sin_l = trig[:, LANES:]

    lane = lax.broadcasted_iota(jnp.int32, (1, LANES), 1) % HEAD_DIM
    s_lo = jnp.where(lane < ROPE_HALF, -sin_l, 0.0)
    s_hi = jnp.where((lane >= ROPE_HALF) & (lane < ROPE_DIM), sin_l, 0.0)
    kcols = []
    for c in range(D_ATTN // LANES):
        kc = ugk[:, 2 * D_RNN + c * LANES: 2 * D_RNN + (c + 1) * LANES]
        kcols.append(kc * cos_l + pltpu.roll(kc, LANES - ROPE_HALF, 1) * s_lo
                     + pltpu.roll(kc, ROPE_HALF, 1) * s_hi)
    k = jnp.concatenate(kcols, axis=1)
    nblk = TM_PROJ // MOBA_BLOCK
    lane_oh = lax.broadcasted_iota(jnp.int32, (MOBA_BLOCK, LANES), 1)
    for t2 in range(nblk):
        rows_t2 = slice(t2 * MOBA_BLOCK, (t2 + 1) * MOBA_BLOCK)
        onehot = (lane_oh == pl.program_id(1) * nblk + t2).astype(BF16)
        pieces = []
        for c in range(N_PAIRS):
            pieces += [kcols[c][rows_t2].astype(BF16), onehot]
        k_ref[rows_t2, :] = jnp.concatenate(pieces, axis=1)
    km_ref[...] = jnp.concatenate(
        [jnp.mean(k[t2 * MOBA_BLOCK:(t2 + 1) * MOBA_BLOCK], axis=0, keepdims=True) for t2 in range(nblk)],
        axis=0)

    qvt = _dot_nt(wqvt_ref[...], h)
    scale = math.log2(math.e) / math.sqrt(HEAD_DIM)
    rows = []
    for hd in range(N_HEADS):
        base = hd * HEAD_DIM
        x1 = qvt[base:base + ROPE_HALF]
        x2 = qvt[base + ROPE_HALF:base + ROPE_DIM]
        rows += [x1 * cos_t - x2 * sin_t, x2 * cos_t + x1 * sin_t, qvt[base + ROPE_DIM:base + HEAD_DIM]]
    qt = (jnp.concatenate(rows, axis=0) * scale).astype(BF16)
    vt = qvt[D_ATTN:].astype(BF16)
    for t2 in range(nblk):
        sl = slice(t2 * MOBA_BLOCK, (t2 + 1) * MOBA_BLOCK)
        qt_ref[:, t2] = qt[:, sl].reshape(N_PAIRS, LANES, MOBA_BLOCK)
        vt_ref[:, t2] = vt[:, sl].reshape(N_PAIRS, LANES, MOBA_BLOCK)


def _inproj_call(x1, mod, positions, inv_freq, w_ugk, w_qvt, conv_w, conv_b, w_gate, b_gate, lam, norm_g):
    bsz, seq, d = x1.shape
    nt = TM_PROJ // MOBA_BLOCK
    nblk = seq // MOBA_BLOCK
    lane = jnp.arange(LANES) % HEAD_DIM
    rotary_lane = lane < ROPE_DIM
    route = ((jnp.arange(ROPE_HALF)[:, None] == (lane % ROPE_HALF)[None, :]) & rotary_lane[None, :]).astype(F32)
    none = jnp.zeros_like(route)
    ones_row = jnp.zeros_like(route).at[0].set(jnp.where(rotary_lane, 0.0, 1.0))
    expand = jnp.concatenate(
        [jnp.concatenate([route, none], axis=1)] * 3 + [jnp.concatenate([none, route], axis=1)] * 3
        + [jnp.concatenate([ones_row, none], axis=1), jnp.concatenate([none, none], axis=1)], axis=0).astype(BF16)
    pos_row = positions.reshape(bsz, seq // TM_PROJ, 1, TM_PROJ)
    tok512 = pl.BlockSpec((None, TM_PROJ, D_RNN), lambda b, t: (b, t, 0))
    blk5 = pl.BlockSpec((None, N_PAIRS, nt, LANES, MOBA_BLOCK), lambda b, t: (b, 0, t, 0, 0))
    shape5 = (bsz, N_PAIRS, nblk, LANES, MOBA_BLOCK)
    return pl.pallas_call(
        _inproj_kernel,
        grid=(bsz, seq // TM_PROJ),
        in_specs=[pl.BlockSpec((None, TM_PROJ, d), lambda b, t: (b, t, 0)),
                  pl.BlockSpec((None, N_MOD, d), lambda b, t: (b, 0, 0)),
                  pl.BlockSpec((None, None, 1, TM_PROJ), lambda b, t: (b, t, 0, 0)),
                  _resident((ROPE_HALF, 1)), _resident((8 * ROPE_HALF, 2 * LANES)),
                  _resident((d, 3 * D_RNN)), _resident((2 * D_ATTN, d)),
                  _resident((CONV_WIDTH, D_RNN)), _resident((1, D_RNN)),
                  _resident((D_RNN, 2 * D_RNN)), _resident((1, 2 * D_RNN)),
                  _resident((1, D_RNN)), _resident((1, D_RNN))],
        out_specs=[tok512,
                   pl.BlockSpec((None, TM_PROJ, N_PAIRS * K_AUG), lambda b, t: (b, t, 0)),
                   pl.BlockSpec((None, None, nt, D_ATTN), lambda b, t: (b, t, 0, 0)),
                   blk5, blk5],
        out_shape=[jax.ShapeDtypeStruct((bsz, seq, D_RNN), BF16),
                   jax.ShapeDtypeStruct((bsz, seq, N_PAIRS * K_AUG), BF16),
                   jax.ShapeDtypeStruct((bsz, seq // TM_PROJ, nt, D_ATTN), F32),
                   jax.ShapeDtypeStruct(shape5, BF16),
                   jax.ShapeDtypeStruct(shape5, BF16)],
        scratch_shapes=[pltpu.VMEM((8, D_RNN), F32), pltpu.VMEM((8, D_RNN), F32)],
        compiler_params=_cparams(2),
        name="inproj",
    )(x1, mod, pos_row, inv_freq.reshape(ROPE_HALF, 1), expand, w_ugk, w_qvt,
      conv_w, conv_b, w_gate, b_gate, lam, norm_g)


def _rglru_tile(u, g, cw_ref, cb_ref, wgate_ref, bgate_ref, lam_ref, ng_ref, tail_ref, h_ref):
    tt = u.shape[0]
    sub = 8
    ext = jnp.concatenate([tail_ref[...], u], axis=0)
    tail_ref[...] = u[tt - sub:tt]
    cw = cw_ref[...]
    uc = u * cw[3:4] + cb_ref[...]
    for back in range(1, CONV_WIDTH):
        uc = uc + pltpu.roll(ext, back, 0)[sub:] * cw[3 - back:4 - back]

    gates = _dot(uc.astype(BF16), wgate_ref[...]) + bgate_ref[...]
    r = _sigmoid(gates[:, :D_RNN])
    ig = _sigmoid(gates[:, D_RNN:])
    z = -lam_ref[...]
    softplus = jnp.maximum(z, 0.0) + jnp.log1p(jnp.exp(-jnp.abs(z)))
    log_a = (-LRU_C) * r * softplus
    a = jnp.exp(log_a)
    one_m_a2 = 1.0 - a * a
    b = (one_m_a2 * lax.rsqrt(jnp.maximum(one_m_a2, 1e-30))) * (ig * uc)

    a = a.reshape(tt // sub, sub, D_RNN)
    b = b.reshape(tt // sub, sub, D_RNN)
    row_in_grp = lax.broadcasted_iota(jnp.int32, (1, sub, 1), 1)
    d = 1
    while d < sub:
        keep = row_in_grp >= d
        a_prev = jnp.where(keep, pltpu.roll(a, d, 1), 1.0)
        b_prev = jnp.where(keep, pltpu.roll(b, d, 1), 0.0)
        b = a * b_prev + b
        a = a * a_prev
        d *= 2
    a = a.reshape(tt, D_RNN)
    b = b.reshape(tt, D_RNN)
    carry = h_ref[0:1, :]
    groups = []
    for grp in range(tt // sub):
        hg = a[grp * sub:(grp + 1) * sub] * carry + b[grp * sub:(grp + 1) * sub]
        groups.append(hg)
        carry = hg[sub - 1:sub]
    h = jnp.concatenate(groups, axis=0)
    h_ref[...] = jnp.broadcast_to(carry, h_ref.shape)

    gelu = 0.5 * g * (1.0 + jnp.tanh(math.sqrt(2.0 / math.pi) * (g + 0.044715 * (g * g * g))))
    y = h * gelu
    ms = jnp.mean(y * y, axis=-1, keepdims=True)
    return y * lax.rsqrt(ms + RMS_EPS) * ng_ref[...]


def _top3_bias(gate, qblk):
    nblk = gate.shape[0]
    jidx = lax.broadcasted_iota(jnp.int32, gate.shape, 0).astype(F32)
    own = qblk.astype(F32)
    valid = jidx < own
    g = jnp.where(valid, gate, NEG_INF)
    sel = jnp.zeros(gate.shape, jnp.bool_)
    for _ in range(MOBA_TOPK):
        mx = jnp.max(g, axis=0, keepdims=True)
        first = jnp.min(jnp.where(g == mx, jidx, float(nblk)), axis=0, keepdims=True)
        pick = jidx == first
        sel = sel | pick
        g = jnp.where(pick, -jnp.inf, g)
    return jnp.where((sel & valid) | (jidx == own), 0.0, NEG_INF)


def _moba_kernel(qt_ref, k_ref, vt_ref, km_ref, o_ref, s0_ref, s1_ref, mx0_ref, mx1_ref, m_ref, acc_ref):
    npr, ntl = qt_ref.shape[:2]
    i0 = pl.program_id(2) * ntl
    nblk = km_ref.shape[0]
    ckeys = MOBA_CHUNK * MOBA_BLOCK
    acc_rows = acc_ref.shape[1]
    streams = [(pr, tl, hd) for pr in range(npr) for tl in range(ntl) for hd in range(2)]

    q_aug = []
    for pr in range(npr):
        km = km_ref[:, pr * LANES:(pr + 1) * LANES]
        lane_km = lax.broadcasted_iota(jnp.int32, km.shape, 1)
        km2 = jnp.concatenate([jnp.where(lane_km < HEAD_DIM, km, 0.0),
                               jnp.where(lane_km >= HEAD_DIM, km, 0.0)], axis=0).astype(BF16)
        for tl in range(ntl):
            qt = qt_ref[pr, tl]
            gate = _dot(km2, qt)
            feat = lax.broadcasted_iota(jnp.int32, qt.shape, 0)
            for hd in range(2):
                q_head = jnp.where((feat >= HEAD_DIM) if hd else (feat < HEAD_DIM), qt, jnp.zeros_like(qt))
                bias = _top3_bias(gate[hd * nblk:(hd + 1) * nblk], i0 + tl).astype(BF16)
                q_aug.append(jnp.concatenate(
                    [q_head, bias, jnp.zeros((K_AUG - LANES - nblk, MOBA_BLOCK), BF16)], axis=0))

    key_in_blk = lax.broadcasted_iota(jnp.int32, (MOBA_BLOCK, MOBA_BLOCK), 0)
    qry_in_blk = lax.broadcasted_iota(jnp.int32, (MOBA_BLOCK, MOBA_BLOCK), 1)
    ones_rows = jnp.ones((acc_rows - HEAD_DIM, MOBA_BLOCK), BF16)

    def scores(c, dst, group):
        s_dst, mx_dst = dst
        rows = pl.ds(pl.multiple_of(c * ckeys, ckeys), ckeys)
        for n in group:
            pr = streams[n][0]
            s = _dot(k_ref[rows, pr * K_AUG:(pr + 1) * K_AUG], q_aug[n])
            s_dst[n] = s
            mx_dst[n] = jnp.max(s.reshape(ckeys // 8, 8, MOBA_BLOCK), axis=0)

    def softmax_pv(c, src, causal, group, nblocks=MOBA_CHUNK):
        s_src, mx_src = src
        for n in group:
            pr, tl, hd = streams[n]
            m_old = m_ref[n]

            def piece(r):
                s = s_src[n, r * MOBA_BLOCK:(r + 1) * MOBA_BLOCK, :]
                if causal:
                    blk_delta = c * MOBA_CHUNK + r - (i0 + tl)
                    s = jnp.where(key_in_blk + blk_delta * MOBA_BLOCK <= qry_in_blk, s, NEG_INF)
                return s

            if causal:
                mx = piece(0)
                for r in range(1, nblocks):
                    mx = jnp.maximum(mx, piece(r))
            else:
                mx = mx_src[n]
            m_new = jnp.maximum(m_old, jnp.max(mx, axis=0, keepdims=True))
            acc = jnp.exp2(m_old - m_new) * acc_ref[n]
            for r in range(nblocks):
                p = jnp.exp2(piece(r) - m_new).astype(BF16)
                vt_aug = jnp.concatenate(
                    [vt_ref[pr, c * MOBA_CHUNK + r, hd * HEAD_DIM:(hd + 1) * HEAD_DIM, :], ones_rows], axis=0)
                acc = acc + _dot(vt_aug, p)
            m_ref[n] = m_new
            acc_ref[n] = acc

    def step(c, s_cur, s_nxt, group):
        scores(c + 1, s_nxt, group)
        softmax_pv(c, s_cur, False, group)

    m_ref[...] = jnp.full(m_ref.shape, NEG_INF, F32)
    acc_ref[...] = jnp.zeros(acc_ref.shape, F32)
    last = lax.shift_right_logical(i0, MOBA_CHUNK_LOG2)
    buf0, buf1 = (s0_ref, mx0_ref), (s1_ref, mx1_ref)
    group = list(range(len(streams)))
    scores(0, buf0, group)

    def two_steps(u, _):
        step(2 * u, buf0, buf1, group)
        step(2 * u + 1, buf1, buf0, group)
        return 0

    lax.fori_loop(0, lax.shift_right_logical(last, 1), two_steps, 0)

    used = (i0 & (MOBA_CHUNK - 1)) + ntl
    for nblocks in range(ntl, MOBA_CHUNK + 1, ntl):
        @pl.when(((last & 1) == 1) & (used == nblocks))
        def _(nblocks=nblocks):
            step(last - 1, buf0, buf1, group)
            softmax_pv(last, buf1, True, group, nblocks)

        @pl.when(((last & 1) == 0) & (used == nblocks))
        def _(nblocks=nblocks):
            softmax_pv(last, buf0, True, group, nblocks)

    for n in range(0, len(streams), 2):
        pr, tl, _ = streams[n]
        o_ref[pr, tl] = jnp.concatenate(
            [acc_ref[n + hd, :HEAD_DIM, :] / acc_ref[n + hd, HEAD_DIM:HEAD_DIM + 1, :] for hd in range(2)],
            axis=0).astype(o_ref.dtype)


def _moba_call(qt, k, vt, kmean):
    bsz, npair, nblk, _, _ = qt.shape
    seq = k.shape[1]
    npr, ntl = MOBA_PAIRS_PER_STEP, MOBA_QBLOCKS_PER_STEP
    n_streams = 2 * npr * ntl
    qblk = pl.BlockSpec((None, npr, ntl, LANES, MOBA_BLOCK), lambda b, p, i: (b, p, i, 0, 0))
    return pl.pallas_call(
        _moba_kernel,
        grid=(bsz, npair // npr, nblk // ntl),
        in_specs=[qblk,
                  pl.BlockSpec((None, seq, npr * K_AUG), lambda b, p, i: (b, 0, p)),
                  pl.BlockSpec((None, npr, nblk, LANES, MOBA_BLOCK), lambda b, p, i: (b, p, 0, 0, 0)),
                  pl.BlockSpec((None, nblk, npr * LANES), lambda b, p, i: (b, 0, p))],
        out_specs=qblk,
        out_shape=jax.ShapeDtypeStruct(qt.shape, BF16),
        scratch_shapes=[pltpu.VMEM((n_streams, MOBA_CHUNK * MOBA_BLOCK, MOBA_BLOCK), F32),
                        pltpu.VMEM((n_streams, MOBA_CHUNK * MOBA_BLOCK, MOBA_BLOCK), F32),
                        pltpu.VMEM((n_streams, 8, MOBA_BLOCK), F32),
                        pltpu.VMEM((n_streams, 8, MOBA_BLOCK), F32),
                        pltpu.VMEM((n_streams, 1, MOBA_BLOCK), F32),
                        pltpu.VMEM((n_streams, HEAD_DIM + 16, MOBA_BLOCK), F32)],
        compiler_params=_cparams(3),
        name="moba",
    )(qt, k, vt, kmean)


def _block_diag(w):
    n, d, e = w.shape
    eye = jnp.eye(n, dtype=w.dtype)
    return (eye[:, None, :, None] * w[:, :, None, :]).reshape(n * d, n * e)


def kernel(x, c, positions, ada_w, ada_b, ffn1_w_gate, ffn1_w_up, ffn1_w_down, ln1_g, ln1_b, w_in, conv_w, conv_b, lru_wa, lru_ba, lru_wx, lru_bx, lru_lambda, norm_rnn_g, norm_attn_g, w_out, ln2_g, ln2_b, ffn2_w_gate, ffn2_w_up, ffn2_w_down, ln3_g, ln3_b):
    bsz, seq, d = x.shape
    depth = ada_w.shape[0]
    inv_freq = ROPE_THETA ** (-jnp.arange(ROPE_HALF, dtype=F32) / ROPE_HALF)
    row = lambda v: v.reshape(1, -1)
    for l in range(depth):
        mod = _mod_call(c, ada_w[l], ada_b[l]).reshape(bsz, N_MOD, d)
        x1 = _ffn1_call(x, mod, ffn1_w_gate[l].astype(BF16), ffn1_w_up[l].astype(BF16),
                        ffn1_w_down[l].astype(BF16), row(ln1_g[l]), row(ln1_b[l]))

        w_ugk = jnp.concatenate([w_in[l][:, :2 * D_RNN], w_in[l][:, 2 * D_RNN + D_ATTN:2 * D_RNN + 2 * D_ATTN]],
                                axis=1).astype(BF16)
        w_qvt = jnp.concatenate([w_in[l][:, 2 * D_RNN:2 * D_RNN + D_ATTN], w_in[l][:, 2 * D_RNN + 2 * D_ATTN:]],
                                axis=1).T.astype(BF16)
        w_gate = jnp.concatenate([_block_diag(lru_wa[l]), _block_diag(lru_wx[l])], axis=1).astype(BF16)
        b_gate = jnp.concatenate([lru_ba[l], lru_bx[l]]).reshape(1, -1)
        yr, k, kmean, qt, vt = _inproj_call(x1, mod, positions, inv_freq, w_ugk, w_qvt, conv_w[l], row(conv_b[l]),
                                            w_gate, b_gate, row(lru_lambda[l]), row(norm_rnn_g[l]))

        yat = _moba_call(qt, k, vt, kmean.reshape(bsz, seq // MOBA_BLOCK, D_ATTN))

        x = _ffn2_call(x1, yr, yat, mod, w_out[l][:D_RNN].astype(BF16), w_out[l][D_RNN:].astype(BF16),
                       norm_attn_g[l].reshape(D_ATTN, 1), row(ln2_g[l]), row(ln2_b[l]),
                       ffn2_w_gate[l].astype(BF16), ffn2_w_up[l].astype(BF16), ffn2_w_down[l].astype(BF16),
                       row(ln3_g[l]), row(ln3_b[l]))
    return x
```

```python
import functools
import math

import jax
import jax.numpy as jnp
from jax import lax
from jax.experimental import pallas as pl
from jax.experimental.pallas import tpu as pltpu

F32 = jnp.float32
BF16 = jnp.bfloat16

D_MODEL = 1024
D_RNN = 512
D_ATTN = 512
HEAD_DIM = 64
N_HEADS = D_ATTN // HEAD_DIM
N_PAIRS = N_HEADS // 2
N_RNN_BLOCKS = D_RNN // HEAD_DIM
CONV_WIDTH = 4
LRU_C = 8.0
MOBA_BLOCK = 256
MOBA_TOPK = 3
MOBA_CHUNK_LOG2 = 2
MOBA_CHUNK = 1 << MOBA_CHUNK_LOG2
MOBA_PAIRS_PER_STEP = 2
MOBA_QBLOCKS_PER_STEP = 2
K_AUG = 256
ROPE_THETA = 500000.0
ROPE_DIM = HEAD_DIM // 4
ROPE_HALF = ROPE_DIM // 2
D_FF = 2816
FF_CHUNKS = ((0, 1024), (1024, 1024), (2048, 768))
N_MOD = 9
ALPHA = 2.0 ** 0.25
LN_EPS = 1e-5
RMS_EPS = 1e-6
NEG_INF = -1e30

LANES = 128
VMEM_LIMIT_BYTES = 56 * 1024 * 1024

TM_FFN = 512
TM_PROJ = 512


def _dot(a, b):
    return jnp.dot(a, b, preferred_element_type=F32)


def _dot_nt(a, b):
    return lax.dot_general(a, b, (((1,), (1,)), ((), ())), preferred_element_type=F32)


def _dot_tn(a, b):
    return lax.dot_general(a, b, (((0,), (0,)), ((), ())), preferred_element_type=F32)


def _layer_norm(v, g, b):
    mu = jnp.mean(v, axis=-1, keepdims=True)
    c = v - mu
    var = jnp.mean(c * c, axis=-1, keepdims=True)
    return c * lax.rsqrt(var + LN_EPS) * g + b


def _sigmoid(v):
    return 0.5 * jnp.tanh(0.5 * v) + 0.5


def _resident(shape):
    nd = len(shape)
    return pl.BlockSpec(shape, lambda *_: (0,) * nd, pipeline_mode=pl.Buffered(1))


def _cparams(n_grid):
    return pltpu.CompilerParams(dimension_semantics=("arbitrary",) * n_grid,
                                vmem_limit_bytes=VMEM_LIMIT_BYTES)


def _mod_kernel(c_ref, w_ref, b_ref, o_ref):
    c = c_ref[...]
    a = c * _sigmoid(c)
    a_hi = a.astype(BF16)
    a_lo = (a - a_hi.astype(F32)).astype(BF16)
    w = w_ref[...]
    w_hi = w.astype(BF16)
    w_lo = (w - w_hi.astype(F32)).astype(BF16)
    o_ref[...] = _dot(a_hi, w_hi) + _dot(a_hi, w_lo) + _dot(a_lo, w_hi) + b_ref[...]


def _mod_call(c, ada_w, ada_b):
    bsz = c.shape[0]
    n_out = ada_w.shape[1]
    tn = D_MODEL
    return pl.pallas_call(
        _mod_kernel,
        grid=(n_out // tn,),
        in_specs=[pl.BlockSpec((bsz, D_MODEL), lambda j: (0, 0)),
                  pl.BlockSpec((D_MODEL, tn), lambda j: (0, j)),
                  pl.BlockSpec((1, tn), lambda j: (0, j))],
        out_specs=pl.BlockSpec((bsz, tn), lambda j: (0, j)),
        out_shape=jax.ShapeDtypeStruct((bsz, n_out), F32),
        compiler_params=_cparams(1),
        name="mod",
    )(c, ada_w, ada_b.reshape(1, n_out))


def _ffn_sublayer(x, sh, sc, gt, wg_ref, wu_ref, wd_ref, ln_g, ln_b):
    h = (x * (1.0 + sc) + sh).astype(BF16)
    acc = jnp.zeros(x.shape, F32)
    for start, size in FF_CHUNKS:
        a = _dot(h, wg_ref[:, start:start + size])
        up = _dot(h, wu_ref[:, start:start + size])
        act = (a * _sigmoid(a) * up).astype(BF16)
        acc = acc + _dot(act, wd_ref[start:start + size, :])
    return _layer_norm(ALPHA * x + (0.5 * (1.0 + gt)) * acc, ln_g, ln_b)


def _ffn1_kernel(x_ref, mod_ref, wg_ref, wu_ref, wd_ref, lng_ref, lnb_ref, o_ref):
    m = mod_ref[...]
    o_ref[...] = _ffn_sublayer(x_ref[...], m[0:1], m[1:2], m[2:3], wg_ref, wu_ref, wd_ref,
                               lng_ref[...], lnb_ref[...])


def _ffn1_call(x, mod, wg, wu, wd, ln_g, ln_b):
    bsz, seq, d = x.shape
    tok = pl.BlockSpec((None, TM_FFN, d), lambda b, t: (b, t, 0))
    return pl.pallas_call(
        _ffn1_kernel,
        grid=(bsz, seq // TM_FFN),
        in_specs=[tok,
                  pl.BlockSpec((None, N_MOD, d), lambda b, t: (b, 0, 0)),
                  _resident((d, D_FF)), _resident((d, D_FF)), _resident((D_FF, d)),
                  _resident((1, d)), _resident((1, d))],
        out_specs=tok,
        out_shape=jax.ShapeDtypeStruct(x.shape, F32),
        compiler_params=_cparams(2),
        name="ffn1",
    )(x, mod, wg, wu, wd, ln_g, ln_b)


def _ffn2_kernel(x_ref, yr_ref, yat_ref, mod_ref, wor_ref, woa_ref, nag_ref, ln2g_ref, ln2b_ref,
                 wg_ref, wu_ref, wd_ref, ln3g_ref, ln3b_ref, o_ref):
    m = mod_ref[...]
    x1 = x_ref[...]
    parts = []
    for t2 in range(TM_FFN // MOBA_BLOCK):
        yt = yat_ref[:, t2].reshape(D_ATTN, MOBA_BLOCK).astype(F32)
        ms = jnp.mean(yt * yt, axis=0, keepdims=True)
        yn = (yt * lax.rsqrt(ms + RMS_EPS) * nag_ref[...]).astype(BF16)
        parts.append(_dot_tn(yn, woa_ref[...]))
    y = _dot(yr_ref[...], wor_ref[...]) + jnp.concatenate(parts, axis=0)
    x2 = _layer_norm(ALPHA * x1 + (1.0 + m[5:6]) * y, ln2g_ref[...], ln2b_ref[...])
    o_ref[...] = _ffn_sublayer(x2, m[6:7], m[7:8], m[8:9], wg_ref, wu_ref, wd_ref,
                               ln3g_ref[...], ln3b_ref[...])


def _ffn2_call(x1, yr, yat, mod, wo_r, wo_a, na_g, ln2_g, ln2_b, wg, wu, wd, ln3_g, ln3_b):
    bsz, seq, d = x1.shape
    nt = TM_FFN // MOBA_BLOCK
    tok = pl.BlockSpec((None, TM_FFN, d), lambda b, t: (b, t, 0))
    return pl.pallas_call(
        _ffn2_kernel,
        grid=(bsz, seq // TM_FFN),
        in_specs=[tok,
                  pl.BlockSpec((None, TM_FFN, D_RNN), lambda b, t: (b, t, 0)),
                  pl.BlockSpec((None, N_PAIRS, nt, LANES, MOBA_BLOCK), lambda b, t: (b, 0, t, 0, 0)),
                  pl.BlockSpec((None, N_MOD, d), lambda b, t: (b, 0, 0)),
                  _resident((D_RNN, d)), _resident((D_ATTN, d)), _resident((D_ATTN, 1)),
                  _resident((1, d)), _resident((1, d)),
                  _resident((d, D_FF)), _resident((d, D_FF)), _resident((D_FF, d)),
                  _resident((1, d)), _resident((1, d))],
        out_specs=tok,
        out_shape=jax.ShapeDtypeStruct(x1.shape, F32),
        compiler_params=_cparams(2),
        name="ffn2",
    )(x1, yr, yat, mod, wo_r, wo_a, na_g, ln2_g, ln2_b, wg, wu, wd, ln3_g, ln3_b)


def _inproj_kernel(x_ref, mod_ref, posr_ref, invc_ref, expand_ref, wugk_ref, wqvt_ref,
                   cw_ref, cb_ref, wgate_ref, bgate_ref, lam_ref, ng_ref,
                   yr_ref, k_ref, km_ref, qt_ref, vt_ref, tail_ref, hstate_ref):
    @pl.when(pl.program_id(1) == 0)
    def _():
        tail_ref[...] = jnp.zeros(tail_ref.shape, F32)
        hstate_ref[...] = jnp.zeros(hstate_ref.shape, F32)

    m = mod_ref[...]
    h = (x_ref[...] * (1.0 + m[4:5]) + m[3:4]).astype(BF16)
    ugk = _dot(h, wugk_ref[...])
    yr_ref[...] = _rglru_tile(ugk[:, :D_RNN], ugk[:, D_RNN:2 * D_RNN], cw_ref, cb_ref, wgate_ref, bgate_ref,
                              lam_ref, ng_ref, tail_ref, hstate_ref).astype(yr_ref.dtype)

    ang_t = invc_ref[...] * posr_ref[...].astype(F32)
    cos_t = jnp.cos(ang_t)
    sin_t = jnp.sin(ang_t)

    def bf16_terms(v):
        hi = v.astype(BF16).astype(F32)
        mid = (v - hi).astype(BF16).astype(F32)
        return [hi, mid, v - hi - mid]

    terms = bf16_terms(cos_t) + bf16_terms(sin_t) + [jnp.ones_like(cos_t), jnp.zeros_like(cos_t)]
    trig = _dot_tn(jnp.concatenate(terms, axis=0).astype(BF16), expand_ref[...])
    cos_l = trig[:, :LANES]
    sin_l = trig[:, LANES:]

    lane = lax.broadcasted_iota(jnp.int32, (1, LANES), 1) % HEAD_DIM
    s_lo = jnp.where(lane < ROPE_HALF, -sin_l, 0.0)
    s_hi = jnp.where((lane >= ROPE_HALF) & (lane < ROPE_DIM), sin_l, 0.0)
    kcols = []
    for c in range(D_ATTN // LANES):
        kc = ugk[:, 2 * D_RNN + c * LANES: 2 * D_RNN + (c + 1) * LANES]
        kcols.append(kc * cos_l + pltpu.roll(kc, LANES - ROPE_HALF, 1) * s_lo
                     + pltpu.roll(kc, ROPE_HALF, 1) * s_hi)
    k = jnp.concatenate(kcols, axis=1)
    nblk = TM_PROJ // MOBA_BLOCK
    lane_oh = lax.broadcasted_iota(jnp.int32, (MOBA_BLOCK, LANES), 1)
    for t2 in range(nblk):
        rows_t2 = slice(t2 * MOBA_BLOCK, (t2 + 1) * MOBA_BLOCK)
        onehot = (lane_oh == pl.program_id(1) * nblk + t2).astype(BF16)
        pieces = []
        for c in range(N_PAIRS):
            pieces += [kcols[c][rows_t2].astype(BF16), onehot]
        k_ref[rows_t2, :] = jnp.concatenate(pieces, axis=1)
    km_ref[...] = jnp.concatenate(
        [jnp.mean(k[t2 * MOBA_BLOCK:(t2 + 1) * MOBA_BLOCK], axis=0, keepdims=True) for t2 in range(nblk)],
        axis=0)

    qvt = _dot_nt(wqvt_ref[...], h)
    scale = math.log2(math.e) / math.sqrt(HEAD_DIM)
    rows = []
    for hd in range(N_HEADS):
        base = hd * HEAD_DIM
        x1 = qvt[base:base + ROPE_HALF]
        x2 = qvt[base + ROPE_HALF:base + ROPE_DIM]
        rows += [x1 * cos_t - x2 * sin_t, x2 * cos_t + x1 * sin_t, qvt[base + ROPE_DIM:base + HEAD_DIM]]
    qt = (jnp.concatenate(rows, axis=0) * scale).astype(BF16)
    vt = qvt[D_ATTN:].astype(BF16)
    for t2 in range(nblk):
        sl = slice(t2 * MOBA_BLOCK, (t2 + 1) * MOBA_BLOCK)
        qt_ref[:, t2] = qt[:, sl].reshape(N_PAIRS, LANES, MOBA_BLOCK)
        vt_ref[:, t2] = vt[:, sl].reshape(N_PAIRS, LANES, MOBA_BLOCK)


def _inproj_call(x1, mod, positions, inv_freq, w_ugk, w_qvt, conv_w, conv_b, w_gate, b_gate, lam, norm_g):
    bsz, seq, d = x1.shape
    nt = TM_PROJ // MOBA_BLOCK
    nblk = seq // MOBA_BLOCK
    lane = jnp.arange(LANES) % HEAD_DIM
    rotary_lane = lane < ROPE_DIM
    route = ((jnp.arange(ROPE_HALF)[:, None] == (lane % ROPE_HALF)[None, :]) & rotary_lane[None, :]).astype(F32)
    none = jnp.zeros_like(route)
    ones_row = jnp.zeros_like(route).at[0].set(jnp.where(rotary_lane, 0.0, 1.0))
    expand = jnp.concatenate(
        [jnp.concatenate([route, none], axis=1)] * 3 + [jnp.concatenate([none, route], axis=1)] * 3
        + [jnp.concatenate([ones_row, none], axis=1), jnp.concatenate([none, none], axis=1)], axis=0).astype(BF16)
    pos_row = positions.reshape(bsz, seq // TM_PROJ, 1, TM_PROJ)
    tok512 = pl.BlockSpec((None, TM_PROJ, D_RNN), lambda b, t: (b, t, 0))
    blk5 = pl.BlockSpec((None, N_PAIRS, nt, LANES, MOBA_BLOCK), lambda b, t: (b, 0, t, 0, 0))
    shape5 = (bsz, N_PAIRS, nblk, LANES, MOBA_BLOCK)
    return pl.pallas_call(
        _inproj_kernel,
        grid=(bsz, seq // TM_PROJ),
        in_specs=[pl.BlockSpec((None, TM_PROJ, d), lambda b, t: (b, t, 0)),
                  pl.BlockSpec((None, N_MOD, d), lambda b, t: (b, 0, 0)),
                  pl.BlockSpec((None, None, 1, TM_PROJ), lambda b, t: (b, t, 0, 0)),
                  _resident((ROPE_HALF, 1)), _resident((8 * ROPE_HALF, 2 * LANES)),
                  _resident((d, 3 * D_RNN)), _resident((2 * D_ATTN, d)),
                  _resident((CONV_WIDTH, D_RNN)), _resident((1, D_RNN)),
                  _resident((D_RNN, 2 * D_RNN)), _resident((1, 2 * D_RNN)),
                  _resident((1, D_RNN)), _resident((1, D_RNN))],
        out_specs=[tok512,
                   pl.BlockSpec((None, TM_PROJ, N_PAIRS * K_AUG), lambda b, t: (b, t, 0)),
                   pl.BlockSpec((None, None, nt, D_ATTN), lambda b, t: (b, t, 0, 0)),
                   blk5, blk5],
        out_shape=[jax.ShapeDtypeStruct((bsz, seq, D_RNN), BF16),
                   jax.ShapeDtypeStruct((bsz, seq, N_PAIRS * K_AUG), BF16),
                   jax.ShapeDtypeStruct((bsz, seq // TM_PROJ, nt, D_ATTN), F32),
                   jax.ShapeDtypeStruct(shape5, BF16),
                   jax.ShapeDtypeStruct(shape5, BF16)],
        scratch_shapes=[pltpu.VMEM((8, D_RNN), F32), pltpu.VMEM((8, D_RNN), F32)],
        compiler_params=_cparams(2),
        name="inproj",
    )(x1, mod, pos_row, inv_freq.reshape(ROPE_HALF, 1), expand, w_ugk, w_qvt,
      conv_w, conv_b, w_gate, b_gate, lam, norm_g)


def _rglru_tile(u, g, cw_ref, cb_ref, wgate_ref, bgate_ref, lam_ref, ng_ref, tail_ref, h_ref):
    tt = u.shape[0]
    sub = 8
    ext = jnp.concatenate([tail_ref[...], u], axis=0)
    tail_ref[...] = u[tt - sub:tt]
    cw = cw_ref[...]
    uc = u * cw[3:4] + cb_ref[...]
    for back in range(1, CONV_WIDTH):
        uc = uc + pltpu.roll(ext, back, 0)[sub:] * cw[3 - back:4 - back]

    gates = _dot(uc.astype(BF16), wgate_ref[...]) + bgate_ref[...]
    r = _sigmoid(gates[:, :D_RNN])
    ig = _sigmoid(gates[:, D_RNN:])
    z = -lam_ref[...]
    softplus = jnp.maximum(z, 0.0) + jnp.log1p(jnp.exp(-jnp.abs(z)))
    log_a = (-LRU_C) * r * softplus
    a = jnp.exp(log_a)
    one_m_a2 = 1.0 - a * a
    b = (one_m_a2 * lax.rsqrt(jnp.maximum(one_m_a2, 1e-30))) * (ig * uc)

    a = a.reshape(tt // sub, sub, D_RNN)
    b = b.reshape(tt // sub, sub, D_RNN)
    row_in_grp = lax.broadcasted_iota(jnp.int32, (1, sub, 1), 1)
    d = 1
    while d < sub:
        keep = row_in_grp >= d
        a_prev = jnp.where(keep, pltpu.roll(a, d, 1), 1.0)
        b_prev = jnp.where(keep, pltpu.roll(b, d, 1), 0.0)
        b = a * b_prev + b
        a = a * a_prev
        d *= 2
    a = a.reshape(tt, D_RNN)
    b = b.reshape(tt, D_RNN)
    carry = h_ref[0:1, :]
    groups = []
    for grp in range(tt // sub):
        hg = a[grp * sub:(grp + 1) * sub] * carry + b[grp * sub:(grp + 1) * sub]
        groups.append(hg)
        carry = hg[sub - 1:sub]
    h = jnp.concatenate(groups, axis=0)
    h_ref[...] = jnp.broadcast_to(carry, h_ref.shape)

    gelu = 0.5 * g * (1.0 + jnp.tanh(math.sqrt(2.0 / math.pi) * (g + 0.044715 * (g * g * g))))
    y = h * gelu
    ms = jnp.mean(y * y, axis=-1, keepdims=True)
    return y * lax.rsqrt(ms + RMS_EPS) * ng_ref[...]


def _top3_bias(gate, qblk):
    nblk = gate.shape[0]
    jidx = lax.broadcasted_iota(jnp.int32, gate.shape, 0).astype(F32)
    own = qblk.astype(F32)
    valid = jidx < own
    g = jnp.where(valid, gate, NEG_INF)
    sel = jnp.zeros(gate.shape, jnp.bool_)
    for _ in range(MOBA_TOPK):
        mx = jnp.max(g, axis=0, keepdims=True)
        first = jnp.min(jnp.where(g == mx, jidx, float(nblk)), axis=0, keepdims=True)
        pick = jidx == first
        sel = sel | pick
        g = jnp.where(pick, -jnp.inf, g)
    return jnp.where((sel & valid) | (jidx == own), 0.0, NEG_INF)


def _moba_kernel(qt_ref, k_ref, vt_ref, km_ref, o_ref, s0_ref, s1_ref, mx0_ref, mx1_ref, m_ref, acc_ref):
    npr, ntl = qt_ref.shape[:2]
    i0 = pl.program_id(2) * ntl
    nblk = km_ref.shape[0]
    ckeys = MOBA_CHUNK * MOBA_BLOCK
    acc_rows = acc_ref.shape[1]
    streams = [(pr, tl, hd) for pr in range(npr) for tl in range(ntl) for hd in range(2)]

    q_aug = []
    for pr in range(npr):
        km = km_ref[:, pr * LANES:(pr + 1) * LANES]
        lane_km = lax.broadcasted_iota(jnp.int32, km.shape, 1)
        km2 = jnp.concatenate([jnp.where(lane_km < HEAD_DIM, km, 0.0),
                               jnp.where(lane_km >= HEAD_DIM, km, 0.0)], axis=0).astype(BF16)
        for tl in range(ntl):
            qt = qt_ref[pr, tl]
            gate = _dot(km2, qt)
            feat = lax.broadcasted_iota(jnp.int32, qt.shape, 0)
            for hd in range(2):
                q_head = jnp.where((feat >= HEAD_DIM) if hd else (feat < HEAD_DIM), qt, jnp.zeros_like(qt))
                bias = _top3_bias(gate[hd * nblk:(hd + 1) * nblk], i0 + tl).astype(BF16)
                q_aug.append(jnp.concatenate(
                    [q_head, bias, jnp.zeros((K_AUG - LANES - nblk, MOBA_BLOCK), BF16)], axis=0))

    key_in_blk = lax.broadcasted_iota(jnp.int32, (MOBA_BLOCK, MOBA_BLOCK), 0)
    qry_in_blk = lax.broadcasted_iota(jnp.int32, (MOBA_BLOCK, MOBA_BLOCK), 1)
    ones_rows = jnp.ones((acc_rows - HEAD_DIM, MOBA_BLOCK), BF16)

    def scores(c, dst, group, nblocks=MOBA_CHUNK):
        s_dst, mx_dst = dst
        nkeys = nblocks * MOBA_BLOCK
        rows = pl.ds(pl.multiple_of(c * ckeys, ckeys), nkeys)
        for n in group:
            pr = streams[n][0]
            s = _dot(k_ref[rows, pr * K_AUG:(pr + 1) * K_AUG], q_aug[n])
            s_dst[n, 0:nkeys, :] = s
            mx_dst[n] = jnp.max(s.reshape(nkeys // 8, 8, MOBA_BLOCK), axis=0)

    def softmax_pv(c, src, own_blocks, group):
        s_src, mx_src = src
        for n in group:
            pr, tl, hd = streams[n]
            m_old = m_ref[n]
            own = None if own_blocks is None else own_blocks - ntl + tl
            visited = MOBA_CHUNK if own is None else own + 1

            def piece(r):
                s = s_src[n, r * MOBA_BLOCK:(r + 1) * MOBA_BLOCK, :]
                if r == own:
                    s = jnp.where(key_in_blk <= qry_in_blk, s, NEG_INF)
                return s

            if own is None:
                mx = mx_src[n]
            else:
                mx = piece(0)
                for r in range(1, visited):
                    mx = jnp.maximum(mx, piece(r))
            m_new = jnp.maximum(m_old, jnp.max(mx, axis=0, keepdims=True))
            acc = jnp.exp2(m_old - m_new) * acc_ref[n]
            for r in range(visited):
                p = jnp.exp2(piece(r) - m_new).astype(BF16)
                vt_aug = jnp.concatenate(
                    [vt_ref[pr, c * MOBA_CHUNK + r, hd * HEAD_DIM:(hd + 1) * HEAD_DIM, :], ones_rows], axis=0)
                acc = acc + _dot(vt_aug, p)
            m_ref[n] = m_new
            acc_ref[n] = acc

    def step(c, s_cur, s_nxt, group, next_blocks=MOBA_CHUNK):
        scores(c + 1, s_nxt, group, next_blocks)
        softmax_pv(c, s_cur, None, group)

    m_ref[...] = jnp.full(m_ref.shape, NEG_INF, F32)
    acc_ref[...] = jnp.zeros(acc_ref.shape, F32)
    last = lax.shift_right_logical(i0, MOBA_CHUNK_LOG2)
    buf0, buf1 = (s0_ref, mx0_ref), (s1_ref, mx1_ref)
    group = list(range(len(streams)))
    scores(0, buf0, group)

    def two_steps(u, _):
        step(2 * u, buf0, buf1, group)
        step(2 * u + 1, buf1, buf0, group)
        return 0

    lax.fori_loop(0, lax.shift_right_logical(last, 1), two_steps, 0)

    used = (i0 & (MOBA_CHUNK - 1)) + ntl
    for nblocks in range(ntl, MOBA_CHUNK + 1, ntl):
        @pl.when(((last & 1) == 1) & (used == nblocks))
        def _(nblocks=nblocks):
            step(last - 1, buf0, buf1, group, nblocks)
            softmax_pv(last, buf1, nblocks, group)

        @pl.when(((last & 1) == 0) & (used == nblocks))
        def _(nblocks=nblocks):
            softmax_pv(last, buf0, nblocks, group)

    for n in range(0, len(streams), 2):
        pr, tl, _ = streams[n]
        o_ref[pr, tl] = jnp.concatenate(
            [acc_ref[n + hd, :HEAD_DIM, :] / acc_ref[n + hd, HEAD_DIM:HEAD_DIM + 1, :] for hd in range(2)],
            axis=0).astype(o_ref.dtype)


def _moba_call(qt, k, vt, kmean):
    bsz, npair, nblk, _, _ = qt.shape
    seq = k.shape[1]
    npr, ntl = MOBA_PAIRS_PER_STEP, MOBA_QBLOCKS_PER_STEP
    n_streams = 2 * npr * ntl
    qblk = pl.BlockSpec((None, npr, ntl, LANES, MOBA_BLOCK), lambda b, p, i: (b, p, i, 0, 0))
    return pl.pallas_call(
        _moba_kernel,
        grid=(bsz, npair // npr, nblk // ntl),
        in_specs=[qblk,
                  pl.BlockSpec((None, seq, npr * K_AUG), lambda b, p, i: (b, 0, p)),
                  pl.BlockSpec((None, npr, nblk, LANES, MOBA_BLOCK), lambda b, p, i: (b, p, 0, 0, 0)),
                  pl.BlockSpec((None, nblk, npr * LANES), lambda b, p, i: (b, 0, p))],
        out_specs=qblk,
        out_shape=jax.ShapeDtypeStruct(qt.shape, BF16),
        scratch_shapes=[pltpu.VMEM((n_streams, MOBA_CHUNK * MOBA_BLOCK, MOBA_BLOCK), F32),
                        pltpu.VMEM((n_streams, MOBA_CHUNK * MOBA_BLOCK, MOBA_BLOCK), F32),
                        pltpu.VMEM((n_streams, 8, MOBA_BLOCK), F32),
                        pltpu.VMEM((n_streams, 8, MOBA_BLOCK), F32),
                        pltpu.VMEM((n_streams, 1, MOBA_BLOCK), F32),
                        pltpu.VMEM((n_streams, HEAD_DIM + 16, MOBA_BLOCK), F32)],
        compiler_params=_cparams(3),
        name="moba",
    )(qt, k, vt, kmean)


def _block_diag(w):
    n, d, e = w.shape
    eye = jnp.eye(n, dtype=w.dtype)
    return (eye[:, None, :, None] * w[:, :, None, :]).reshape(n * d, n * e)


def kernel(x, c, positions, ada_w, ada_b, ffn1_w_gate, ffn1_w_up, ffn1_w_down, ln1_g, ln1_b, w_in, conv_w, conv_b, lru_wa, lru_ba, lru_wx, lru_bx, lru_lambda, norm_rnn_g, norm_attn_g, w_out, ln2_g, ln2_b, ffn2_w_gate, ffn2_w_up, ffn2_w_down, ln3_g, ln3_b):
    bsz, seq, d = x.shape
    depth = ada_w.shape[0]
    inv_freq = ROPE_THETA ** (-jnp.arange(ROPE_HALF, dtype=F32) / ROPE_HALF)
    row = lambda v: v.reshape(1, -1)
    for l in range(depth):
        mod = _mod_call(c, ada_w[l], ada_b[l]).reshape(bsz, N_MOD, d)
        x1 = _ffn1_call(x, mod, ffn1_w_gate[l].astype(BF16), ffn1_w_up[l].astype(BF16),
                        ffn1_w_down[l].astype(BF16), row(ln1_g[l]), row(ln1_b[l]))

        w_ugk = jnp.concatenate([w_in[l][:, :2 * D_RNN], w_in[l][:, 2 * D_RNN + D_ATTN:2 * D_RNN + 2 * D_ATTN]],
                                axis=1).astype(BF16)
        w_qvt = jnp.concatenate([w_in[l][:, 2 * D_RNN:2 * D_RNN + D_ATTN], w_in[l][:, 2 * D_RNN + 2 * D_ATTN:]],
                                axis=1).T.astype(BF16)
        w_gate = jnp.concatenate([_block_diag(lru_wa[l]), _block_diag(lru_wx[l])], axis=1).astype(BF16)
        b_gate = jnp.concatenate([lru_ba[l], lru_bx[l]]).reshape(1, -1)
        yr, k, kmean, qt, vt = _inproj_call(x1, mod, positions, inv_freq, w_ugk, w_qvt, conv_w[l], row(conv_b[l]),
                                            w_gate, b_gate, row(lru_lambda[l]), row(norm_rnn_g[l]))

        yat = _moba_call(qt, k, vt, kmean.reshape(bsz, seq // MOBA_BLOCK, D_ATTN))

        x = _ffn2_call(x1, yr, yat, mod, w_out[l][:D_RNN].astype(BF16), w_out[l][D_RNN:].astype(BF16),
                       norm_attn_g[l].reshape(D_ATTN, 1), row(ln2_g[l]), row(ln2_b[l]),
                       ffn2_w_gate[l].astype(BF16), ffn2_w_up[l].astype(BF16), ffn2_w_down[l].astype(BF16),
                       row(ln3_g[l]), row(ln3_b[l]))
    return x
```

```python
import math

import jax
import jax.numpy as jnp
from jax import lax
from jax.experimental import pallas as pl
from jax.experimental.pallas import tpu as pltpu

F32 = jnp.float32
BF16 = jnp.bfloat16

D_MODEL = 1024
D_RNN = 512
D_ATTN = 512
HEAD_DIM = 64
N_HEADS = D_ATTN // HEAD_DIM
N_PAIRS = N_HEADS // 2
CONV_WIDTH = 4
LRU_C = 8.0
MOBA_BLOCK = 256
MOBA_TOPK = 3
MOBA_CHUNK_LOG2 = 2
MOBA_CHUNK = 1 << MOBA_CHUNK_LOG2
MOBA_PAIRS_PER_STEP = 2
MOBA_QBLOCKS_PER_STEP = 2
K_AUG = 256
ROPE_THETA = 500000.0
ROPE_DIM = HEAD_DIM // 4
ROPE_HALF = ROPE_DIM // 2
D_FF = 2816
FF_CHUNKS = ((0, 1024), (1024, 1024), (2048, 768))
N_MOD = 9
ALPHA = 2.0 ** 0.25
LN_EPS = 1e-5
RMS_EPS = 1e-6
NEG_INF = -1e30

LANES = 128
VMEM_LIMIT_BYTES = 56 * 1024 * 1024

TM_FFN = 512
TM_PROJ = 512


def _dot(a, b):
    return jnp.dot(a, b, preferred_element_type=F32)


def _dot_nt(a, b):
    return lax.dot_general(a, b, (((1,), (1,)), ((), ())), preferred_element_type=F32)


def _dot_tn(a, b):
    return lax.dot_general(a, b, (((0,), (0,)), ((), ())), preferred_element_type=F32)


def _layer_norm(v, g, b):
    mu = jnp.mean(v, axis=-1, keepdims=True)
    c = v - mu
    var = jnp.mean(c * c, axis=-1, keepdims=True)
    return c * lax.rsqrt(var + LN_EPS) * g + b


def _sigmoid(v):
    return 0.5 * jnp.tanh(0.5 * v) + 0.5


def _resident(shape):
    nd = len(shape)
    return pl.BlockSpec(shape, lambda *_: (0,) * nd, pipeline_mode=pl.Buffered(1))


def _cparams(n_grid):
    return pltpu.CompilerParams(dimension_semantics=("arbitrary",) * n_grid,
                                vmem_limit_bytes=VMEM_LIMIT_BYTES)


def _mod_kernel(c_ref, w_ref, b_ref, o_ref):
    c = c_ref[...]
    a = c * _sigmoid(c)
    a_hi = a.astype(BF16)
    a_lo = (a - a_hi.astype(F32)).astype(BF16)
    w = w_ref[...]
    w_hi = w.astype(BF16)
    w_lo = (w - w_hi.astype(F32)).astype(BF16)
    o_ref[...] = _dot(a_hi, w_hi) + _dot(a_hi, w_lo) + _dot(a_lo, w_hi) + b_ref[...]


def _mod_call(c, ada_w, ada_b):
    bsz = c.shape[0]
    n_out = ada_w.shape[1]
    tn = D_MODEL
    return pl.pallas_call(
        _mod_kernel,
        grid=(n_out // tn,),
        in_specs=[pl.BlockSpec((bsz, D_MODEL), lambda j: (0, 0)),
                  pl.BlockSpec((D_MODEL, tn), lambda j: (0, j)),
                  pl.BlockSpec((1, tn), lambda j: (0, j))],
        out_specs=pl.BlockSpec((bsz, tn), lambda j: (0, j)),
        out_shape=jax.ShapeDtypeStruct((bsz, n_out), F32),
        compiler_params=_cparams(1),
        name="mod",
    )(c, ada_w, ada_b.reshape(1, n_out))


def _ffn_sublayer(x, sh, sc, gt, wg_ref, wu_ref, wd_ref, ln_g, ln_b):
    h = (x * (1.0 + sc) + sh).astype(BF16)
    acc = jnp.zeros(x.shape, F32)
    for start, size in FF_CHUNKS:
        a = _dot(h, wg_ref[:, start:start + size])
        up = _dot(h, wu_ref[:, start:start + size])
        act = (a * _sigmoid(a) * up).astype(BF16)
        acc = acc + _dot(act, wd_ref[start:start + size, :])
    return _layer_norm(ALPHA * x + (0.5 * (1.0 + gt)) * acc, ln_g, ln_b)


def _ffn1_kernel(x_ref, mod_ref, wg_ref, wu_ref, wd_ref, lng_ref, lnb_ref, o_ref):
    m = mod_ref[...]
    o_ref[...] = _ffn_sublayer(x_ref[...], m[0:1], m[1:2], m[2:3], wg_ref, wu_ref, wd_ref,
                               lng_ref[...], lnb_ref[...])


def _ffn1_call(x, mod, wg, wu, wd, ln_g, ln_b):
    bsz, seq, d = x.shape
    tok = pl.BlockSpec((None, TM_FFN, d), lambda b, t: (b, t, 0))
    return pl.pallas_call(
        _ffn1_kernel,
        grid=(bsz, seq // TM_FFN),
        in_specs=[tok,
                  pl.BlockSpec((None, N_MOD, d), lambda b, t: (b, 0, 0)),
                  _resident((d, D_FF)), _resident((d, D_FF)), _resident((D_FF, d)),
                  _resident((1, d)), _resident((1, d))],
        out_specs=tok,
        out_shape=jax.ShapeDtypeStruct(x.shape, F32),
        compiler_params=_cparams(2),
        name="ffn1",
    )(x, mod, wg, wu, wd, ln_g, ln_b)


def _ffn2_kernel(x_ref, yr_ref, yat_ref, mod_ref, wor_ref, woa_ref, nag_ref, ln2g_ref, ln2b_ref,
                 wg_ref, wu_ref, wd_ref, ln3g_ref, ln3b_ref, o_ref):
    m = mod_ref[...]
    x1 = x_ref[...]
    parts = []
    for t2 in range(TM_FFN // MOBA_BLOCK):
        yt = yat_ref[:, t2].reshape(D_ATTN, MOBA_BLOCK).astype(F32)
        ms = jnp.mean(yt * yt, axis=0, keepdims=True)
        yn = (yt * lax.rsqrt(ms + RMS_EPS) * nag_ref[...]).astype(BF16)
        parts.append(_dot_tn(yn, woa_ref[...]))
    y = _dot(yr_ref[...], wor_ref[...]) + jnp.concatenate(parts, axis=0)
    x2 = _layer_norm(ALPHA * x1 + (1.0 + m[5:6]) * y, ln2g_ref[...], ln2b_ref[...])
    o_ref[...] = _ffn_sublayer(x2, m[6:7], m[7:8], m[8:9], wg_ref, wu_ref, wd_ref,
                               ln3g_ref[...], ln3b_ref[...])


def _ffn2_call(x1, yr, yat, mod, wo_r, wo_a, na_g, ln2_g, ln2_b, wg, wu, wd, ln3_g, ln3_b):
    bsz, seq, d = x1.shape
    nt = TM_FFN // MOBA_BLOCK
    tok = pl.BlockSpec((None, TM_FFN, d), lambda b, t: (b, t, 0))
    return pl.pallas_call(
        _ffn2_kernel,
        grid=(bsz, seq // TM_FFN),
        in_specs=[tok,
                  pl.BlockSpec((None, TM_FFN, D_RNN), lambda b, t: (b, t, 0)),
                  pl.BlockSpec((None, N_PAIRS, nt, LANES, MOBA_BLOCK), lambda b, t: (b, 0, t, 0, 0)),
                  pl.BlockSpec((None, N_MOD, d), lambda b, t: (b, 0, 0)),
                  _resident((D_RNN, d)), _resident((D_ATTN, d)), _resident((D_ATTN, 1)),
                  _resident((1, d)), _resident((1, d)),
                  _resident((d, D_FF)), _resident((d, D_FF)), _resident((D_FF, d)),
                  _resident((1, d)), _resident((1, d))],
        out_specs=tok,
        out_shape=jax.ShapeDtypeStruct(x1.shape, F32),
        compiler_params=_cparams(2),
        name="ffn2",
    )(x1, yr, yat, mod, wo_r, wo_a, na_g, ln2_g, ln2_b, wg, wu, wd, ln3_g, ln3_b)


def _inproj_kernel(x_ref, mod_ref, posr_ref, invc_ref, expand_ref, wugk_ref, wqvt_ref,
                   cw_ref, cb_ref, wgate_ref, bgate_ref, lam_ref, ng_ref,
                   yr_ref, k_ref, km_ref, qt_ref, vt_ref, tail_ref, hstate_ref):
    @pl.when(pl.program_id(1) == 0)
    def _():
        tail_ref[...] = jnp.zeros(tail_ref.shape, F32)
        hstate_ref[...] = jnp.zeros(hstate_ref.shape, F32)

    m = mod_ref[...]
    h = (x_ref[...] * (1.0 + m[4:5]) + m[3:4]).astype(BF16)
    ugk = _dot(h, wugk_ref[...])
    yr_ref[...] = _rglru_tile(ugk[:, :D_RNN], ugk[:, D_RNN:2 * D_RNN], cw_ref, cb_ref, wgate_ref, bgate_ref,
                              lam_ref, ng_ref, tail_ref, hstate_ref).astype(yr_ref.dtype)

    ang_t = invc_ref[...] * posr_ref[...].astype(F32)
    cos_t = jnp.cos(ang_t)
    sin_t = jnp.sin(ang_t)

    def bf16_terms(v):
        hi = v.astype(BF16).astype(F32)
        mid = (v - hi).astype(BF16).astype(F32)
        return [hi, mid, v - hi - mid]

    terms = bf16_terms(cos_t) + bf16_terms(sin_t) + [jnp.ones_like(cos_t), jnp.zeros_like(cos_t)]
    trig = _dot_tn(jnp.concatenate(terms, axis=0).astype(BF16), expand_ref[...])
    cos_l = trig[:, :LANES]
    sin_l = trig[:, LANES:]

    lane = lax.broadcasted_iota(jnp.int32, (1, LANES), 1) % HEAD_DIM
    s_lo = jnp.where(lane < ROPE_HALF, -sin_l, 0.0)
    s_hi = jnp.where((lane >= ROPE_HALF) & (lane < ROPE_DIM), sin_l, 0.0)
    kcols = []
    for c in range(D_ATTN // LANES):
        kc = ugk[:, 2 * D_RNN + c * LANES: 2 * D_RNN + (c + 1) * LANES]
        kcols.append(kc * cos_l + pltpu.roll(kc, LANES - ROPE_HALF, 1) * s_lo
                     + pltpu.roll(kc, ROPE_HALF, 1) * s_hi)
    k = jnp.concatenate(kcols, axis=1)
    nblk = TM_PROJ // MOBA_BLOCK
    lane_oh = lax.broadcasted_iota(jnp.int32, (MOBA_BLOCK, LANES), 1)
    for t2 in range(nblk):
        rows_t2 = slice(t2 * MOBA_BLOCK, (t2 + 1) * MOBA_BLOCK)
        onehot = (lane_oh == pl.program_id(1) * nblk + t2).astype(BF16)
        pieces = []
        for c in range(N_PAIRS):
            pieces += [kcols[c][rows_t2].astype(BF16), onehot]
        k_ref[rows_t2, :] = jnp.concatenate(pieces, axis=1)
    km_ref[...] = jnp.concatenate(
        [jnp.mean(k[t2 * MOBA_BLOCK:(t2 + 1) * MOBA_BLOCK], axis=0, keepdims=True) for t2 in range(nblk)],
        axis=0)

    qvt = _dot_nt(wqvt_ref[...], h)
    scale = math.log2(math.e) / math.sqrt(HEAD_DIM)
    rows = []
    for hd in range(N_HEADS):
        base = hd * HEAD_DIM
        x1 = qvt[base:base + ROPE_HALF]
        x2 = qvt[base + ROPE_HALF:base + ROPE_DIM]
        rows += [x1 * cos_t - x2 * sin_t, x2 * cos_t + x1 * sin_t, qvt[base + ROPE_DIM:base + HEAD_DIM]]
    qt = (jnp.concatenate(rows, axis=0) * scale).astype(BF16)
    vt = qvt[D_ATTN:].astype(BF16)
    for t2 in range(nblk):
        sl = slice(t2 * MOBA_BLOCK, (t2 + 1) * MOBA_BLOCK)
        qt_ref[:, t2] = qt[:, sl].reshape(N_PAIRS, LANES, MOBA_BLOCK)
        vt_ref[:, t2] = vt[:, sl].reshape(N_PAIRS, LANES, MOBA_BLOCK)


def _inproj_call(x1, mod, positions, inv_freq, w_ugk, w_qvt, conv_w, conv_b, w_gate, b_gate, lam, norm_g):
    bsz, seq, d = x1.shape
    nt = TM_PROJ // MOBA_BLOCK
    nblk = seq // MOBA_BLOCK
    lane = jnp.arange(LANES) % HEAD_DIM
    rotary_lane = lane < ROPE_DIM
    route = ((jnp.arange(ROPE_HALF)[:, None] == (lane % ROPE_HALF)[None, :]) & rotary_lane[None, :]).astype(F32)
    none = jnp.zeros_like(route)
    ones_row = jnp.zeros_like(route).at[0].set(jnp.where(rotary_lane, 0.0, 1.0))
    expand = jnp.concatenate(
        [jnp.concatenate([route, none], axis=1)] * 3 + [jnp.concatenate([none, route], axis=1)] * 3
        + [jnp.concatenate([ones_row, none], axis=1), jnp.concatenate([none, none], axis=1)], axis=0).astype(BF16)
    pos_row = positions.reshape(bsz, seq // TM_PROJ, 1, TM_PROJ)
    tok512 = pl.BlockSpec((None, TM_PROJ, D_RNN), lambda b, t: (b, t, 0))
    blk5 = pl.BlockSpec((None, N_PAIRS, nt, LANES, MOBA_BLOCK), lambda b, t: (b, 0, t, 0, 0))
    shape5 = (bsz, N_PAIRS, nblk, LANES, MOBA_BLOCK)
    return pl.pallas_call(
        _inproj_kernel,
        grid=(bsz, seq // TM_PROJ),
        in_specs=[pl.BlockSpec((None, TM_PROJ, d), lambda b, t: (b, t, 0)),
                  pl.BlockSpec((None, N_MOD, d), lambda b, t: (b, 0, 0)),
                  pl.BlockSpec((None, None, 1, TM_PROJ), lambda b, t: (b, t, 0, 0)),
                  _resident((ROPE_HALF, 1)), _resident((8 * ROPE_HALF, 2 * LANES)),
                  _resident((d, 3 * D_RNN)), _resident((2 * D_ATTN, d)),
                  _resident((CONV_WIDTH, D_RNN)), _resident((1, D_RNN)),
                  _resident((D_RNN, 2 * D_RNN)), _resident((1, 2 * D_RNN)),
                  _resident((1, D_RNN)), _resident((1, D_RNN))],
        out_specs=[tok512,
                   pl.BlockSpec((None, TM_PROJ, N_PAIRS * K_AUG), lambda b, t: (b, t, 0)),
                   pl.BlockSpec((None, None, nt, D_ATTN), lambda b, t: (b, t, 0, 0)),
                   blk5, blk5],
        out_shape=[jax.ShapeDtypeStruct((bsz, seq, D_RNN), BF16),
                   jax.ShapeDtypeStruct((bsz, seq, N_PAIRS * K_AUG), BF16),
                   jax.ShapeDtypeStruct((bsz, seq // TM_PROJ, nt, D_ATTN), F32),
                   jax.ShapeDtypeStruct(shape5, BF16),
                   jax.ShapeDtypeStruct(shape5, BF16)],
        scratch_shapes=[pltpu.VMEM((8, D_RNN), F32), pltpu.VMEM((8, D_RNN), F32)],
        compiler_params=_cparams(2),
        name="inproj",
    )(x1, mod, pos_row, inv_freq.reshape(ROPE_HALF, 1), expand, w_ugk, w_qvt,
      conv_w, conv_b, w_gate, b_gate, lam, norm_g)


def _rglru_tile(u, g, cw_ref, cb_ref, wgate_ref, bgate_ref, lam_ref, ng_ref, tail_ref, h_ref):
    tt = u.shape[0]
    sub = 8
    ext = jnp.concatenate([tail_ref[...], u], axis=0)
    tail_ref[...] = u[tt - sub:tt]
    cw = cw_ref[...]
    uc = u * cw[3:4] + cb_ref[...]
    for back in range(1, CONV_WIDTH):
        uc = uc + pltpu.roll(ext, back, 0)[sub:] * cw[3 - back:4 - back]

    gates = _dot(uc.astype(BF16), wgate_ref[...]) + bgate_ref[...]
    r = _sigmoid(gates[:, :D_RNN])
    ig = _sigmoid(gates[:, D_RNN:])
    z = -lam_ref[...]
    softplus = jnp.maximum(z, 0.0) + jnp.log1p(jnp.exp(-jnp.abs(z)))
    log_a = (-LRU_C) * r * softplus
    a = jnp.exp(log_a)
    one_m_a2 = 1.0 - a * a
    b = (one_m_a2 * lax.rsqrt(jnp.maximum(one_m_a2, 1e-30))) * (ig * uc)

    a = a.reshape(tt // sub, sub, D_RNN)
    b = b.reshape(tt // sub, sub, D_RNN)
    row_in_grp = lax.broadcasted_iota(jnp.int32, (1, sub, 1), 1)
    d = 1
    while d < sub:
        keep = row_in_grp >= d
        a_prev = jnp.where(keep, pltpu.roll(a, d, 1), 1.0)
        b_prev = jnp.where(keep, pltpu.roll(b, d, 1), 0.0)
        b = a * b_prev + b
        a = a * a_prev
        d *= 2
    a = a.reshape(tt, D_RNN)
    b = b.reshape(tt, D_RNN)
    carry = h_ref[0:1, :]
    groups = []
    for grp in range(tt // sub):
        hg = a[grp * sub:(grp + 1) * sub] * carry + b[grp * sub:(grp + 1) * sub]
        groups.append(hg)
        carry = hg[sub - 1:sub]
    h = jnp.concatenate(groups, axis=0)
    h_ref[...] = jnp.broadcast_to(carry, h_ref.shape)

    gelu = 0.5 * g * (1.0 + jnp.tanh(math.sqrt(2.0 / math.pi) * (g + 0.044715 * (g * g * g))))
    y = h * gelu
    ms = jnp.mean(y * y, axis=-1, keepdims=True)
    return y * lax.rsqrt(ms + RMS_EPS) * ng_ref[...]


def _top3_bias(gate, qblk):
    nblk = gate.shape[0]
    jidx = lax.broadcasted_iota(jnp.int32, gate.shape, 0).astype(F32)
    own = qblk.astype(F32)
    valid = jidx < own
    g = jnp.where(valid, gate, NEG_INF)
    sel = jnp.zeros(gate.shape, jnp.bool_)
    for _ in range(MOBA_TOPK):
        mx = jnp.max(g, axis=0, keepdims=True)
        first = jnp.min(jnp.where(g == mx, jidx, float(nblk)), axis=0, keepdims=True)
        pick = jidx == first
        sel = sel | pick
        g = jnp.where(pick, -jnp.inf, g)
    return jnp.where((sel & valid) | (jidx == own), 0.0, NEG_INF)


def _moba_kernel(qt_ref, k_ref, vt_ref, km_ref, o_ref, s0_ref, s1_ref, mx0_ref, mx1_ref, m_ref, acc_ref):
    npr, ntl = qt_ref.shape[:2]
    i0 = pl.program_id(2) * ntl
    nblk = km_ref.shape[0]
    ckeys = MOBA_CHUNK * MOBA_BLOCK
    acc_rows = acc_ref.shape[1]
    streams = [(pr, tl, hd) for pr in range(npr) for tl in range(ntl) for hd in range(2)]

    q_aug = []
    for pr in range(npr):
        km = km_ref[:, pr * LANES:(pr + 1) * LANES]
        lane_km = lax.broadcasted_iota(jnp.int32, km.shape, 1)
        km2 = jnp.concatenate([jnp.where(lane_km < HEAD_DIM, km, 0.0),
                               jnp.where(lane_km >= HEAD_DIM, km, 0.0)], axis=0).astype(BF16)
        for tl in range(ntl):
            qt = qt_ref[pr, tl]
            gate = _dot(km2, qt)
            feat = lax.broadcasted_iota(jnp.int32, qt.shape, 0)
            for hd in range(2):
                q_head = jnp.where((feat >= HEAD_DIM) if hd else (feat < HEAD_DIM), qt, jnp.zeros_like(qt))
                bias = _top3_bias(gate[hd * nblk:(hd + 1) * nblk], i0 + tl).astype(BF16)
                q_aug.append(jnp.concatenate(
                    [q_head, bias, jnp.zeros((K_AUG - LANES - nblk, MOBA_BLOCK), BF16)], axis=0))

    key_in_blk = lax.broadcasted_iota(jnp.int32, (MOBA_BLOCK, MOBA_BLOCK), 0)
    qry_in_blk = lax.broadcasted_iota(jnp.int32, (MOBA_BLOCK, MOBA_BLOCK), 1)
    ones_rows = jnp.ones((acc_rows - HEAD_DIM, MOBA_BLOCK), BF16)

    def scores(c, dst, group, nblocks=MOBA_CHUNK):
        s_dst, mx_dst = dst
        nkeys = nblocks * MOBA_BLOCK
        rows = pl.ds(pl.multiple_of(c * ckeys, ckeys), nkeys)
        for n in group:
            pr = streams[n][0]
            s = _dot(k_ref[rows, pr * K_AUG:(pr + 1) * K_AUG], q_aug[n])
            s_dst[n, 0:nkeys, :] = s
            mx_dst[n] = jnp.max(s.reshape(nkeys // 8, 8, MOBA_BLOCK), axis=0)

    def softmax_pv(c, src, own_blocks, group):
        s_src, mx_src = src
        for n in group:
            pr, tl, hd = streams[n]
            m_old = m_ref[n]
            own = None if own_blocks is None else own_blocks - ntl + tl
            visited = MOBA_CHUNK if own is None else own + 1

            def piece(r):
                s = s_src[n, r * MOBA_BLOCK:(r + 1) * MOBA_BLOCK, :]
                if r == own:
                    s = jnp.where(key_in_blk <= qry_in_blk, s, NEG_INF)
                return s

            if own is None:
                mx = mx_src[n]
            else:
                mx = piece(0)
                for r in range(1, visited):
                    mx = jnp.maximum(mx, piece(r))
            m_new = jnp.maximum(m_old, jnp.max(mx, axis=0, keepdims=True))
            acc = jnp.exp2(m_old - m_new) * acc_ref[n]
            for r in range(visited):
                p = jnp.exp2(piece(r) - m_new).astype(BF16)
                vt_aug = jnp.concatenate(
                    [vt_ref[pr, c * MOBA_CHUNK + r, hd * HEAD_DIM:(hd + 1) * HEAD_DIM, :], ones_rows], axis=0)
                acc = acc + _dot(vt_aug, p)
            m_ref[n] = m_new
            acc_ref[n] = acc

    def step(c, s_cur, s_nxt, group, next_blocks=MOBA_CHUNK):
        scores(c + 1, s_nxt, group, next_blocks)
        softmax_pv(c, s_cur, None, group)

    m_ref[...] = jnp.full(m_ref.shape, NEG_INF, F32)
    acc_ref[...] = jnp.zeros(acc_ref.shape, F32)
    last = lax.shift_right_logical(i0, MOBA_CHUNK_LOG2)
    buf0, buf1 = (s0_ref, mx0_ref), (s1_ref, mx1_ref)
    group = list(range(len(streams)))
    scores(0, buf0, group)

    def two_steps(u, _):
        step(2 * u, buf0, buf1, group)
        step(2 * u + 1, buf1, buf0, group)
        return 0

    lax.fori_loop(0, lax.shift_right_logical(last, 1), two_steps, 0)

    used = (i0 & (MOBA_CHUNK - 1)) + ntl
    for nblocks in range(ntl, MOBA_CHUNK + 1, ntl):
        @pl.when(((last & 1) == 1) & (used == nblocks))
        def _(nblocks=nblocks):
            step(last - 1, buf0, buf1, group, nblocks)
            softmax_pv(last, buf1, nblocks, group)

        @pl.when(((last & 1) == 0) & (used == nblocks))
        def _(nblocks=nblocks):
            softmax_pv(last, buf0, nblocks, group)

    for n in range(0, len(streams), 2):
        pr, tl, _ = streams[n]
        o_ref[pr, tl] = jnp.concatenate(
            [acc_ref[n + hd, :HEAD_DIM, :] / acc_ref[n + hd, HEAD_DIM:HEAD_DIM + 1, :] for hd in range(2)],
            axis=0).astype(o_ref.dtype)


def _moba_call(qt, k, vt, kmean):
    bsz, npair, nblk, _, _ = qt.shape
    seq = k.shape[1]
    npr, ntl = MOBA_PAIRS_PER_STEP, MOBA_QBLOCKS_PER_STEP
    n_streams = 2 * npr * ntl
    qblk = pl.BlockSpec((None, npr, ntl, LANES, MOBA_BLOCK), lambda b, p, i: (b, p, i, 0, 0))
    return pl.pallas_call(
        _moba_kernel,
        grid=(bsz, npair // npr, nblk // ntl),
        in_specs=[qblk,
                  pl.BlockSpec((None, seq, npr * K_AUG), lambda b, p, i: (b, 0, p)),
                  pl.BlockSpec((None, npr, nblk, LANES, MOBA_BLOCK), lambda b, p, i: (b, p, 0, 0, 0)),
                  pl.BlockSpec((None, nblk, npr * LANES), lambda b, p, i: (b, 0, p))],
        out_specs=qblk,
        out_shape=jax.ShapeDtypeStruct(qt.shape, BF16),
        scratch_shapes=[pltpu.VMEM((n_streams, MOBA_CHUNK * MOBA_BLOCK, MOBA_BLOCK), F32),
                        pltpu.VMEM((n_streams, MOBA_CHUNK * MOBA_BLOCK, MOBA_BLOCK), F32),
                        pltpu.VMEM((n_streams, 8, MOBA_BLOCK), F32),
                        pltpu.VMEM((n_streams, 8, MOBA_BLOCK), F32),
                        pltpu.VMEM((n_streams, 1, MOBA_BLOCK), F32),
                        pltpu.VMEM((n_streams, HEAD_DIM + 16, MOBA_BLOCK), F32)],
        compiler_params=_cparams(3),
        name="moba",
    )(qt, k, vt, kmean)


def _block_diag(w):
    n, d, e = w.shape
    eye = jnp.eye(n, dtype=w.dtype)
    return (eye[:, None, :, None] * w[:, :, None, :]).reshape(n * d, n * e)


def kernel(x, c, positions, ada_w, ada_b, ffn1_w_gate, ffn1_w_up, ffn1_w_down, ln1_g, ln1_b, w_in, conv_w, conv_b, lru_wa, lru_ba, lru_wx, lru_bx, lru_lambda, norm_rnn_g, norm_attn_g, w_out, ln2_g, ln2_b, ffn2_w_gate, ffn2_w_up, ffn2_w_down, ln3_g, ln3_b):
    bsz, seq, d = x.shape
    depth = ada_w.shape[0]
    assert d == D_MODEL and w_in.shape[-1] == 2 * D_RNN + 3 * D_ATTN and ffn1_w_gate.shape[-1] == D_FF
    assert seq % (MOBA_BLOCK * MOBA_CHUNK) == 0 and seq % TM_FFN == 0 and seq % TM_PROJ == 0
    assert seq // MOBA_BLOCK <= LANES, "one-hot key-block columns must fit one lane group"
    inv_freq =ROPE_THETA ** (-jnp.arange(ROPE_HALF, dtype=F32) / ROPE_HALF)
    row = lambda v: v.reshape(1, -1)
    for l in range(depth):
        mod = _mod_call(c, ada_w[l], ada_b[l]).reshape(bsz, N_MOD, d)
        x1 = _ffn1_call(x, mod, ffn1_w_gate[l].astype(BF16), ffn1_w_up[l].astype(BF16),
                        ffn1_w_down[l].astype(BF16), row(ln1_g[l]), row(ln1_b[l]))

        w_ugk = jnp.concatenate([w_in[l][:, :2 * D_RNN], w_in[l][:, 2 * D_RNN + D_ATTN:2 * D_RNN + 2 * D_ATTN]],
                                axis=1).astype(BF16)
        w_qvt = jnp.concatenate([w_in[l][:, 2 * D_RNN:2 * D_RNN + D_ATTN], w_in[l][:, 2 * D_RNN + 2 * D_ATTN:]],
                                axis=1).T.astype(BF16)
        w_gate = jnp.concatenate([_block_diag(lru_wa[l]), _block_diag(lru_wx[l])], axis=1).astype(BF16)
        b_gate = jnp.concatenate([lru_ba[l], lru_bx[l]]).reshape(1, -1)
        yr, k, kmean, qt, vt = _inproj_call(x1, mod, positions, inv_freq, w_ugk, w_qvt, conv_w[l], row(conv_b[l]),
                                            w_gate, b_gate, row(lru_lambda[l]), row(norm_rnn_g[l]))

        yat = _moba_call(qt, k, vt, kmean.reshape(bsz, seq // MOBA_BLOCK, D_ATTN))

        x = _ffn2_call(x1, yr, yat, mod, w_out[l][:D_RNN].astype(BF16), w_out[l][D_RNN:].astype(BF16),
                       norm_attn_g[l].reshape(D_ATTN, 1), row(ln2_g[l]), row(ln2_b[l]),
                       ffn2_w_gate[l].astype(BF16), ffn2_w_up[l].astype(BF16), ffn2_w_down[l].astype(BF16),
                       row(ln3_g[l]), row(ln3_b[l]))
    return x
```

```python
import math

import jax
import jax.numpy as jnp
from jax import lax
from jax.experimental import pallas as pl
from jax.experimental.pallas import tpu as pltpu

F32 = jnp.float32
BF16 = jnp.bfloat16

D_MODEL = 1024
D_RNN = 512
D_ATTN = 512
HEAD_DIM = 64
N_HEADS = D_ATTN // HEAD_DIM
N_PAIRS = N_HEADS // 2
CONV_WIDTH = 4
LRU_C = 8.0
MOBA_BLOCK = 256
MOBA_TOPK = 3
MOBA_CHUNK_LOG2 = 2
MOBA_CHUNK = 1 << MOBA_CHUNK_LOG2
MOBA_PAIRS_PER_STEP = 2
MOBA_QBLOCKS_PER_STEP = 2
K_AUG = 256
ROPE_THETA = 500000.0
ROPE_DIM = HEAD_DIM // 4
ROPE_HALF = ROPE_DIM // 2
D_FF = 2816
FF_CHUNKS = ((0, 1024), (1024, 1024), (2048, 768))
N_MOD = 9
ALPHA = 2.0 ** 0.25
LN_EPS = 1e-5
RMS_EPS = 1e-6
NEG_INF = -1e30

LANES = 128
VMEM_LIMIT_BYTES = 56 * 1024 * 1024

TM_FFN = 512
TM_PROJ = 512


def _dot(a, b):
    return jnp.dot(a, b, preferred_element_type=F32)


def _dot_nt(a, b):
    return lax.dot_general(a, b, (((1,), (1,)), ((), ())), preferred_element_type=F32)


def _dot_tn(a, b):
    return lax.dot_general(a, b, (((0,), (0,)), ((), ())), preferred_element_type=F32)


def _layer_norm(v, g, b):
    mu = jnp.mean(v, axis=-1, keepdims=True)
    c = v - mu
    var = jnp.mean(c * c, axis=-1, keepdims=True)
    return c * lax.rsqrt(var + LN_EPS) * g + b


def _sigmoid(v):
    return 0.5 * jnp.tanh(0.5 * v) + 0.5


def _resident(shape):
    nd = len(shape)
    return pl.BlockSpec(shape, lambda *_: (0,) * nd, pipeline_mode=pl.Buffered(1))


def _cparams(n_grid):
    return pltpu.CompilerParams(dimension_semantics=("arbitrary",) * n_grid,
                                vmem_limit_bytes=VMEM_LIMIT_BYTES)


def _mod_kernel(c_ref, w_ref, b_ref, o_ref):
    c = c_ref[...]
    a = c * _sigmoid(c)
    a_hi = a.astype(BF16)
    a_lo = (a - a_hi.astype(F32)).astype(BF16)
    w = w_ref[...]
    w_hi = w.astype(BF16)
    w_lo = (w - w_hi.astype(F32)).astype(BF16)
    o_ref[...] = _dot(a_hi, w_hi) + _dot(a_hi, w_lo) + _dot(a_lo, w_hi) + b_ref[...]


def _mod_call(c, ada_w, ada_b):
    bsz = c.shape[0]
    n_out = ada_w.shape[1]
    tn = D_MODEL
    return pl.pallas_call(
        _mod_kernel,
        grid=(n_out // tn,),
        in_specs=[pl.BlockSpec((bsz, D_MODEL), lambda j: (0, 0)),
                  pl.BlockSpec((D_MODEL, tn), lambda j: (0, j)),
                  pl.BlockSpec((1, tn), lambda j: (0, j))],
        out_specs=pl.BlockSpec((bsz, tn), lambda j: (0, j)),
        out_shape=jax.ShapeDtypeStruct((bsz, n_out), F32),
        compiler_params=_cparams(1),
        name="mod",
    )(c, ada_w, ada_b.reshape(1, n_out))


def _ffn_sublayer(x, sh, sc, gt, wg_ref, wu_ref, wd_ref, ln_g, ln_b):
    h = (x * (1.0 + sc) + sh).astype(BF16)
    acc = jnp.zeros(x.shape, F32)
    for start, size in FF_CHUNKS:
        a = _dot(h, wg_ref[:, start:start + size])
        up = _dot(h, wu_ref[:, start:start + size])
        act = (a * _sigmoid(a) * up).astype(BF16)
        acc = acc + _dot(act, wd_ref[start:start + size, :])
    return _layer_norm(ALPHA * x + (0.5 * (1.0 + gt)) * acc, ln_g, ln_b)


def _ffn1_kernel(x_ref, mod_ref, wg_ref, wu_ref, wd_ref, lng_ref, lnb_ref, o_ref):
    m = mod_ref[...]
    o_ref[...] = _ffn_sublayer(x_ref[...], m[0:1], m[1:2], m[2:3], wg_ref, wu_ref, wd_ref,
                               lng_ref[...], lnb_ref[...])


def _ffn1_call(x, mod, wg, wu, wd, ln_g, ln_b):
    bsz, seq, d = x.shape
    tok = pl.BlockSpec((None, TM_FFN, d), lambda b, t: (b, t, 0))
    return pl.pallas_call(
        _ffn1_kernel,
        grid=(bsz, seq // TM_FFN),
        in_specs=[tok,
                  pl.BlockSpec((None, N_MOD, d), lambda b, t: (b, 0, 0)),
                  _resident((d, D_FF)), _resident((d, D_FF)), _resident((D_FF, d)),
                  _resident((1, d)), _resident((1, d))],
        out_specs=tok,
        out_shape=jax.ShapeDtypeStruct(x.shape, F32),
        compiler_params=_cparams(2),
        name="ffn1",
    )(x, mod, wg, wu, wd, ln_g, ln_b)


def _ffn2_kernel(x_ref, yr_ref, yat_ref, mod_ref, wor_ref, woa_ref, nag_ref, ln2g_ref, ln2b_ref,
                 wg_ref, wu_ref, wd_ref, ln3g_ref, ln3b_ref, o_ref):
    m = mod_ref[...]
    x1 = x_ref[...]
    parts = []
    for t2 in range(TM_FFN // MOBA_BLOCK):
        yt = yat_ref[:, t2].reshape(D_ATTN, MOBA_BLOCK).astype(F32)
        ms = jnp.mean(yt * yt, axis=0, keepdims=True)
        yn = (yt * lax.rsqrt(ms + RMS_EPS) * nag_ref[...]).astype(BF16)
        parts.append(_dot_tn(yn, woa_ref[...]))
    y = _dot(yr_ref[...], wor_ref[...]) + jnp.concatenate(parts, axis=0)
    x2 = _layer_norm(ALPHA * x1 + (1.0 + m[5:6]) * y, ln2g_ref[...], ln2b_ref[...])
    o_ref[...] = _ffn_sublayer(x2, m[6:7], m[7:8], m[8:9], wg_ref, wu_ref, wd_ref,
                               ln3g_ref[...], ln3b_ref[...])


def _ffn2_call(x1, yr, yat, mod, wo_r, wo_a, na_g, ln2_g, ln2_b, wg, wu, wd, ln3_g, ln3_b):
    bsz, seq, d = x1.shape
    nt = TM_FFN // MOBA_BLOCK
    tok = pl.BlockSpec((None, TM_FFN, d), lambda b, t: (b, t, 0))
    return pl.pallas_call(
        _ffn2_kernel,
        grid=(bsz, seq // TM_FFN),
        in_specs=[tok,
                  pl.BlockSpec((None, TM_FFN, D_RNN), lambda b, t: (b, t, 0)),
                  pl.BlockSpec((None, N_PAIRS, nt, LANES, MOBA_BLOCK), lambda b, t: (b, 0, t, 0, 0)),
                  pl.BlockSpec((None, N_MOD, d), lambda b, t: (b, 0, 0)),
                  _resident((D_RNN, d)), _resident((D_ATTN, d)), _resident((D_ATTN, 1)),
                  _resident((1, d)), _resident((1, d)),
                  _resident((d, D_FF)), _resident((d, D_FF)), _resident((D_FF, d)),
                  _resident((1, d)), _resident((1, d))],
        out_specs=tok,
        out_shape=jax.ShapeDtypeStruct(x1.shape, F32),
        compiler_params=_cparams(2),
        name="ffn2",
    )(x1, yr, yat, mod, wo_r, wo_a, na_g, ln2_g, ln2_b, wg, wu, wd, ln3_g, ln3_b)


def _inproj_kernel(x_ref, mod_ref, posr_ref, invc_ref, expand_ref, wugk_ref, wqvt_ref,
                   cw_ref, cb_ref, wgate_ref, bgate_ref, lam_ref, ng_ref,
                   yr_ref, k_ref, km_ref, qt_ref, vt_ref, tail_ref, hstate_ref):
    @pl.when(pl.program_id(1) == 0)
    def _():
        tail_ref[...] = jnp.zeros(tail_ref.shape, F32)
        hstate_ref[...] = jnp.zeros(hstate_ref.shape, F32)

    m = mod_ref[...]
    h = (x_ref[...] * (1.0 + m[4:5]) + m[3:4]).astype(BF16)
    u = _dot(h, wugk_ref[:, :D_RNN])
    g = _dot(h, wugk_ref[:, D_RNN:2 * D_RNN])
    yr_ref[...] = _rglru_tile(u, g, cw_ref, cb_ref, wgate_ref, bgate_ref,
                              lam_ref, ng_ref, tail_ref, hstate_ref).astype(yr_ref.dtype)
    k_proj = _dot(h, wugk_ref[:, 2 * D_RNN:])

    ang_t = invc_ref[...] * posr_ref[...].astype(F32)
    cos_t = jnp.cos(ang_t)
    sin_t = jnp.sin(ang_t)

    def bf16_terms(v):
        hi = v.astype(BF16).astype(F32)
        mid = (v - hi).astype(BF16).astype(F32)
        return [hi, mid, v - hi - mid]

    terms = bf16_terms(cos_t) + bf16_terms(sin_t) + [jnp.ones_like(cos_t), jnp.zeros_like(cos_t)]
    trig = _dot_tn(jnp.concatenate(terms, axis=0).astype(BF16), expand_ref[...])
    cos_l = trig[:, :LANES]
    sin_l = trig[:, LANES:]

    lane = lax.broadcasted_iota(jnp.int32, (1, LANES), 1) % HEAD_DIM
    s_lo = jnp.where(lane < ROPE_HALF, -sin_l, 0.0)
    s_hi = jnp.where((lane >= ROPE_HALF) & (lane < ROPE_DIM), sin_l, 0.0)
    kcols = []
    for c in range(D_ATTN // LANES):
        kc = k_proj[:, c * LANES:(c + 1) * LANES]
        kcols.append(kc * cos_l + pltpu.roll(kc, LANES - ROPE_HALF, 1) * s_lo
                     + pltpu.roll(kc, ROPE_HALF, 1) * s_hi)
    k = jnp.concatenate(kcols, axis=1)
    nblk = TM_PROJ // MOBA_BLOCK
    lane_oh = lax.broadcasted_iota(jnp.int32, (MOBA_BLOCK, LANES), 1)
    for t2 in range(nblk):
        rows_t2 = slice(t2 * MOBA_BLOCK, (t2 + 1) * MOBA_BLOCK)
        onehot = (lane_oh == pl.program_id(1) * nblk + t2).astype(BF16)
        pieces = []
        for c in range(N_PAIRS):
            pieces += [kcols[c][rows_t2].astype(BF16), onehot]
        k_ref[rows_t2, :] = jnp.concatenate(pieces, axis=1)
    km_ref[...] = jnp.concatenate(
        [jnp.mean(k[t2 * MOBA_BLOCK:(t2 + 1) * MOBA_BLOCK], axis=0, keepdims=True) for t2 in range(nblk)],
        axis=0)

    qvt = _dot_nt(wqvt_ref[...], h)
    scale = math.log2(math.e) / math.sqrt(HEAD_DIM)
    rows = []
    for hd in range(N_HEADS):
        base = hd * HEAD_DIM
        x1 = qvt[base:base + ROPE_HALF]
        x2 = qvt[base + ROPE_HALF:base + ROPE_DIM]
        rows += [x1 * cos_t - x2 * sin_t, x2 * cos_t + x1 * sin_t, qvt[base + ROPE_DIM:base + HEAD_DIM]]
    qt = (jnp.concatenate(rows, axis=0) * scale).astype(BF16)
    vt = qvt[D_ATTN:].astype(BF16)
    for t2 in range(nblk):
        sl = slice(t2 * MOBA_BLOCK, (t2 + 1) * MOBA_BLOCK)
        qt_ref[:, t2] = qt[:, sl].reshape(N_PAIRS, LANES, MOBA_BLOCK)
        vt_ref[:, t2] = vt[:, sl].reshape(N_PAIRS, LANES, MOBA_BLOCK)


def _inproj_call(x1, mod, positions, inv_freq, w_ugk, w_qvt, conv_w, conv_b, w_gate, b_gate, lam, norm_g):
    bsz, seq, d = x1.shape
    nt = TM_PROJ // MOBA_BLOCK
    nblk = seq // MOBA_BLOCK
    lane = jnp.arange(LANES) % HEAD_DIM
    rotary_lane = lane < ROPE_DIM
    route = ((jnp.arange(ROPE_HALF)[:, None] == (lane % ROPE_HALF)[None, :]) & rotary_lane[None, :]).astype(F32)
    none = jnp.zeros_like(route)
    ones_row = jnp.zeros_like(route).at[0].set(jnp.where(rotary_lane, 0.0, 1.0))
    expand = jnp.concatenate(
        [jnp.concatenate([route, none], axis=1)] * 3 + [jnp.concatenate([none, route], axis=1)] * 3
        + [jnp.concatenate([ones_row, none], axis=1), jnp.concatenate([none, none], axis=1)], axis=0).astype(BF16)
    pos_row = positions.reshape(bsz, seq // TM_PROJ, 1, TM_PROJ)
    tok512 = pl.BlockSpec((None, TM_PROJ, D_RNN), lambda b, t: (b, t, 0))
    blk5 = pl.BlockSpec((None, N_PAIRS, nt, LANES, MOBA_BLOCK), lambda b, t: (b, 0, t, 0, 0))
    shape5 = (bsz, N_PAIRS, nblk, LANES, MOBA_BLOCK)
    return pl.pallas_call(
        _inproj_kernel,
        grid=(bsz, seq // TM_PROJ),
        in_specs=[pl.BlockSpec((None, TM_PROJ, d), lambda b, t: (b, t, 0)),
                  pl.BlockSpec((None, N_MOD, d), lambda b, t: (b, 0, 0)),
                  pl.BlockSpec((None, None, 1, TM_PROJ), lambda b, t: (b, t, 0, 0)),
                  _resident((ROPE_HALF, 1)), _resident((8 * ROPE_HALF, 2 * LANES)),
                  _resident((d, 3 * D_RNN)), _resident((2 * D_ATTN, d)),
                  _resident((CONV_WIDTH, D_RNN)), _resident((1, D_RNN)),
                  _resident((D_RNN, 2 * D_RNN)), _resident((1, 2 * D_RNN)),
                  _resident((1, D_RNN)), _resident((1, D_RNN))],
        out_specs=[tok512,
                   pl.BlockSpec((None, TM_PROJ, N_PAIRS * K_AUG), lambda b, t: (b, t, 0)),
                   pl.BlockSpec((None, None, nt, D_ATTN), lambda b, t: (b, t, 0, 0)),
                   blk5, blk5],
        out_shape=[jax.ShapeDtypeStruct((bsz, seq, D_RNN), BF16),
                   jax.ShapeDtypeStruct((bsz, seq, N_PAIRS * K_AUG), BF16),
                   jax.ShapeDtypeStruct((bsz, seq // TM_PROJ, nt, D_ATTN), F32),
                   jax.ShapeDtypeStruct(shape5, BF16),
                   jax.ShapeDtypeStruct(shape5, BF16)],
        scratch_shapes=[pltpu.VMEM((8, D_RNN), F32), pltpu.VMEM((8, D_RNN), F32)],
        compiler_params=_cparams(2),
        name="inproj",
    )(x1, mod, pos_row, inv_freq.reshape(ROPE_HALF, 1), expand, w_ugk, w_qvt,
      conv_w, conv_b, w_gate, b_gate, lam, norm_g)


def _rglru_tile(u, g, cw_ref, cb_ref, wgate_ref, bgate_ref, lam_ref, ng_ref, tail_ref, h_ref):
    tt = u.shape[0]
    sub = 8
    ext = jnp.concatenate([tail_ref[...], u], axis=0)
    tail_ref[...] = u[tt - sub:tt]
    cw = cw_ref[...]
    uc = u * cw[3:4] + cb_ref[...]
    for back in range(1, CONV_WIDTH):
        uc = uc + pltpu.roll(ext, back, 0)[sub:] * cw[3 - back:4 - back]

    gates = _dot(uc.astype(BF16), wgate_ref[...]) + bgate_ref[...]
    r = _sigmoid(gates[:, :D_RNN])
    ig = _sigmoid(gates[:, D_RNN:])
    z = -lam_ref[...]
    softplus = jnp.maximum(z, 0.0) + jnp.log1p(jnp.exp(-jnp.abs(z)))
    log_a = (-LRU_C) * r * softplus
    a = jnp.exp(log_a)
    one_m_a2 = 1.0 - a * a
    b = (one_m_a2 * lax.rsqrt(jnp.maximum(one_m_a2, 1e-30))) * (ig * uc)

    a = a.reshape(tt // sub, sub, D_RNN)
    b = b.reshape(tt // sub, sub, D_RNN)
    row_in_grp = lax.broadcasted_iota(jnp.int32, (1, sub, 1), 1)
    d = 1
    while d < sub:
        keep = row_in_grp >= d
        a_prev = jnp.where(keep, pltpu.roll(a, d, 1), 1.0)
        b_prev = jnp.where(keep, pltpu.roll(b, d, 1), 0.0)
        b = a * b_prev + b
        a = a * a_prev
        d *= 2
    a = a.reshape(tt, D_RNN)
    b = b.reshape(tt, D_RNN)
    carry = h_ref[0:1, :]
    groups = []
    for grp in range(tt // sub):
        hg = a[grp * sub:(grp + 1) * sub] * carry + b[grp * sub:(grp + 1) * sub]
        groups.append(hg)
        carry = hg[sub - 1:sub]
    h = jnp.concatenate(groups, axis=0)
    h_ref[...] = jnp.broadcast_to(carry, h_ref.shape)

    gelu = 0.5 * g * (1.0 + jnp.tanh(math.sqrt(2.0 / math.pi) * (g + 0.044715 * (g * g * g))))
    y = h * gelu
    ms = jnp.mean(y * y, axis=-1, keepdims=True)
    return y * lax.rsqrt(ms + RMS_EPS) * ng_ref[...]


def _top3_bias(gate, qblk):
    nblk = gate.shape[0]
    jidx = lax.broadcasted_iota(jnp.int32, gate.shape, 0).astype(F32)
    own = qblk.astype(F32)
    valid = jidx < own
    g = jnp.where(valid, gate, NEG_INF)
    sel = jnp.zeros(gate.shape, jnp.bool_)
    for _ in range(MOBA_TOPK):
        mx = jnp.max(g, axis=0, keepdims=True)
        first = jnp.min(jnp.where(g == mx, jidx, float(nblk)), axis=0, keepdims=True)
        pick = jidx == first
        sel = sel | pick
        g = jnp.where(pick, -jnp.inf, g)
    return jnp.where((sel & valid) | (jidx == own), 0.0, NEG_INF)


def _moba_kernel(qt_ref, k_ref, vt_ref, km_ref, o_ref, s0_ref, s1_ref, mx0_ref, mx1_ref, m_ref, acc_ref):
    npr, ntl = qt_ref.shape[:2]
    i0 = pl.program_id(2) * ntl
    nblk = km_ref.shape[0]
    ckeys = MOBA_CHUNK * MOBA_BLOCK
    acc_rows = acc_ref.shape[1]
    streams = [(pr, tl, hd) for pr in range(npr) for tl in range(ntl) for hd in range(2)]

    q_aug = []
    for pr in range(npr):
        km = km_ref[:, pr * LANES:(pr + 1) * LANES]
        lane_km = lax.broadcasted_iota(jnp.int32, km.shape, 1)
        km2 = jnp.concatenate([jnp.where(lane_km < HEAD_DIM, km, 0.0),
                               jnp.where(lane_km >= HEAD_DIM, km, 0.0)], axis=0).astype(BF16)
        for tl in range(ntl):
            qt = qt_ref[pr, tl]
            gate = _dot(km2, qt)
            feat = lax.broadcasted_iota(jnp.int32, qt.shape, 0)
            for hd in range(2):
                q_head = jnp.where((feat >= HEAD_DIM) if hd else (feat < HEAD_DIM), qt, jnp.zeros_like(qt))
                bias = _top3_bias(gate[hd * nblk:(hd + 1) * nblk], i0 + tl).astype(BF16)
                q_aug.append(jnp.concatenate(
                    [q_head, bias, jnp.zeros((K_AUG - LANES - nblk, MOBA_BLOCK), BF16)], axis=0))

    key_in_blk = lax.broadcasted_iota(jnp.int32, (MOBA_BLOCK, MOBA_BLOCK), 0)
    qry_in_blk = lax.broadcasted_iota(jnp.int32, (MOBA_BLOCK, MOBA_BLOCK), 1)
    ones_rows = jnp.ones((acc_rows - HEAD_DIM, MOBA_BLOCK), BF16)

    def scores(c, dst, group, nblocks=MOBA_CHUNK):
        s_dst, mx_dst = dst
        nkeys = nblocks * MOBA_BLOCK
        rows = pl.ds(pl.multiple_of(c * ckeys, ckeys), nkeys)
        for n in group:
            pr = streams[n][0]
            s = _dot(k_ref[rows, pr * K_AUG:(pr + 1) * K_AUG], q_aug[n])
            s_dst[n, 0:nkeys, :] = s
            mx_dst[n] = jnp.max(s.reshape(nkeys // 8, 8, MOBA_BLOCK), axis=0)

    def softmax_pv(c, src, own_blocks, group):
        s_src, mx_src = src
        for n in group:
            pr, tl, hd = streams[n]
            m_old = m_ref[n]
            own = None if own_blocks is None else own_blocks - ntl + tl
            visited = MOBA_CHUNK if own is None else own + 1

            def piece(r):
                s = s_src[n, r * MOBA_BLOCK:(r + 1) * MOBA_BLOCK, :]
                if r == own:
                    s = jnp.where(key_in_blk <= qry_in_blk, s, NEG_INF)
                return s

            if own is None:
                mx = mx_src[n]
            else:
                mx = piece(0)
                for r in range(1, visited):
                    mx = jnp.maximum(mx, piece(r))
            m_new = jnp.maximum(m_old, jnp.max(mx, axis=0, keepdims=True))
            acc = jnp.exp2(m_old - m_new) * acc_ref[n]
            for r in range(visited):
                p = jnp.exp2(piece(r) - m_new).astype(BF16)
                vt_aug = jnp.concatenate(
                    [vt_ref[pr, c * MOBA_CHUNK + r, hd * HEAD_DIM:(hd + 1) * HEAD_DIM, :], ones_rows], axis=0)
                acc = acc + _dot(vt_aug, p)
            m_ref[n] = m_new
            acc_ref[n] = acc

    def step(c, s_cur, s_nxt, group, next_blocks=MOBA_CHUNK):
        scores(c + 1, s_nxt, group[:1], next_blocks)
        for idx, n in enumerate(group):
            if idx + 1 < len(group):
                scores(c + 1, s_nxt, group[idx + 1:idx + 2], next_blocks)
            softmax_pv(c, s_cur, None, [n])

    m_ref[...] = jnp.full(m_ref.shape, NEG_INF, F32)
    acc_ref[...] = jnp.zeros(acc_ref.shape, F32)
    last = lax.shift_right_logical(i0, MOBA_CHUNK_LOG2)
    buf0, buf1 = (s0_ref, mx0_ref), (s1_ref, mx1_ref)
    group = list(range(len(streams)))
    scores(0, buf0, group)

    def two_steps(u, _):
        step(2 * u, buf0, buf1, group)
        step(2 * u + 1, buf1, buf0, group)
        return 0

    lax.fori_loop(0, lax.shift_right_logical(last, 1), two_steps, 0)

    used = (i0 & (MOBA_CHUNK - 1)) + ntl
    for nblocks in range(ntl, MOBA_CHUNK + 1, ntl):
        @pl.when(((last & 1) == 1) & (used == nblocks))
        def _(nblocks=nblocks):
            step(last - 1, buf0, buf1, group, nblocks)
            softmax_pv(last, buf1, nblocks, group)

        @pl.when(((last & 1) == 0) & (used == nblocks))
        def _(nblocks=nblocks):
            softmax_pv(last, buf0, nblocks, group)

    for n in range(0, len(streams), 2):
        pr, tl, _ = streams[n]
        o_ref[pr, tl] = jnp.concatenate(
            [acc_ref[n + hd, :HEAD_DIM, :] / acc_ref[n + hd, HEAD_DIM:HEAD_DIM + 1, :] for hd in range(2)],
            axis=0).astype(o_ref.dtype)


def _moba_call(qt, k, vt, kmean):
    bsz, npair, nblk, _, _ = qt.shape
    seq = k.shape[1]
    npr, ntl = MOBA_PAIRS_PER_STEP, MOBA_QBLOCKS_PER_STEP
    n_streams = 2 * npr * ntl
    qblk = pl.BlockSpec((None, npr, ntl, LANES, MOBA_BLOCK), lambda b, p, i: (b, p, i, 0, 0))
    return pl.pallas_call(
        _moba_kernel,
        grid=(bsz, npair // npr, nblk // ntl),
        in_specs=[qblk,
                  pl.BlockSpec((None, seq, npr * K_AUG), lambda b, p, i: (b, 0, p)),
                  pl.BlockSpec((None, npr, nblk, LANES, MOBA_BLOCK), lambda b, p, i: (b, p, 0, 0, 0)),
                  pl.BlockSpec((None, nblk, npr * LANES), lambda b, p, i: (b, 0, p))],
        out_specs=qblk,
        out_shape=jax.ShapeDtypeStruct(qt.shape, BF16),
        scratch_shapes=[pltpu.VMEM((n_streams, MOBA_CHUNK * MOBA_BLOCK, MOBA_BLOCK), F32),
                        pltpu.VMEM((n_streams, MOBA_CHUNK * MOBA_BLOCK, MOBA_BLOCK), F32),
                        pltpu.VMEM((n_streams, 8, MOBA_BLOCK), F32),
                        pltpu.VMEM((n_streams, 8, MOBA_BLOCK), F32),
                        pltpu.VMEM((n_streams, 1, MOBA_BLOCK), F32),
                        pltpu.VMEM((n_streams, HEAD_DIM + 16, MOBA_BLOCK), F32)],
        compiler_params=_cparams(3),
        name="moba",
    )(qt, k, vt, kmean)


def _block_diag(w):
    n, d, e = w.shape
    eye = jnp.eye(n, dtype=w.dtype)
    return (eye[:, None, :, None] * w[:, :, None, :]).reshape(n * d, n * e)


def kernel(x, c, positions, ada_w, ada_b, ffn1_w_gate, ffn1_w_up, ffn1_w_down, ln1_g, ln1_b, w_in, conv_w, conv_b, lru_wa, lru_ba, lru_wx, lru_bx, lru_lambda, norm_rnn_g, norm_attn_g, w_out, ln2_g, ln2_b, ffn2_w_gate, ffn2_w_up, ffn2_w_down, ln3_g, ln3_b):
    bsz, seq, d = x.shape
    depth = ada_w.shape[0]
    assert d == D_MODEL and w_in.shape[-1] == 2 * D_RNN + 3 * D_ATTN and ffn1_w_gate.shape[-1] == D_FF
    assert seq % (MOBA_BLOCK * MOBA_CHUNK) == 0 and seq % TM_FFN == 0 and seq % TM_PROJ == 0
    assert seq // MOBA_BLOCK <= LANES, "one-hot key-block columns must fit one lane group"
    inv_freq =ROPE_THETA ** (-jnp.arange(ROPE_HALF, dtype=F32) / ROPE_HALF)
    row = lambda v: v.reshape(1, -1)
    for l in range(depth):
        mod = _mod_call(c, ada_w[l], ada_b[l]).reshape(bsz, N_MOD, d)
        x1 = _ffn1_call(x, mod, ffn1_w_gate[l].astype(BF16), ffn1_w_up[l].astype(BF16),
                        ffn1_w_down[l].astype(BF16), row(ln1_g[l]), row(ln1_b[l]))

        w_ugk = jnp.concatenate([w_in[l][:, :2 * D_RNN], w_in[l][:, 2 * D_RNN + D_ATTN:2 * D_RNN + 2 * D_ATTN]],
                                axis=1).astype(BF16)
        w_qvt = jnp.concatenate([w_in[l][:, 2 * D_RNN:2 * D_RNN + D_ATTN], w_in[l][:, 2 * D_RNN + 2 * D_ATTN:]],
                                axis=1).T.astype(BF16)
        w_gate = jnp.concatenate([_block_diag(lru_wa[l]), _block_diag(lru_wx[l])], axis=1).astype(BF16)
        b_gate = jnp.concatenate([lru_ba[l], lru_bx[l]]).reshape(1, -1)
        yr, k, kmean, qt, vt = _inproj_call(x1, mod, positions, inv_freq, w_ugk, w_qvt, conv_w[l], row(conv_b[l]),
                                            w_gate, b_gate, row(lru_lambda[l]), row(norm_rnn_g[l]))

        yat = _moba_call(qt, k, vt, kmean.reshape(bsz, seq // MOBA_BLOCK, D_ATTN))

        x = _ffn2_call(x1, yr, yat, mod, w_out[l][:D_RNN].astype(BF16), w_out[l][D_RNN:].astype(BF16),
                       norm_attn_g[l].reshape(D_ATTN, 1), row(ln2_g[l]), row(ln2_b[l]),
                       ffn2_w_gate[l].astype(BF16), ffn2_w_up[l].astype(BF16), ffn2_w_down[l].astype(BF16),
                       row(ln3_g[l]), row(ln3_b[l]))
    return x
```

```python
import math

import jax
import jax.numpy as jnp
from jax import lax
from jax.experimental import pallas as pl
from jax.experimental.pallas import tpu as pltpu

F32 = jnp.float32
BF16 = jnp.bfloat16

D_MODEL = 1024
D_RNN = 512
D_ATTN = 512
HEAD_DIM = 64
N_HEADS = D_ATTN // HEAD_DIM
N_PAIRS = N_HEADS // 2
CONV_WIDTH = 4
LRU_C = 8.0
MOBA_BLOCK = 256
MOBA_TOPK = 3
MOBA_CHUNK_LOG2 = 2
MOBA_CHUNK = 1 << MOBA_CHUNK_LOG2
MOBA_PAIRS_PER_STEP = 2
MOBA_QBLOCKS_PER_STEP = 2
K_AUG = 256
ROPE_THETA = 500000.0
ROPE_DIM = HEAD_DIM // 4
ROPE_HALF = ROPE_DIM // 2
D_FF = 2816
FF_CHUNKS = ((0, 1024), (1024, 1024), (2048, 768))
N_MOD = 9
ALPHA = 2.0 ** 0.25
LN_EPS = 1e-5
RMS_EPS = 1e-6
NEG_INF = -1e30

LANES = 128
VMEM_LIMIT_BYTES = 56 * 1024 * 1024

TM_FFN = 512
TM_PROJ = 512


def _dot(a, b):
    return jnp.dot(a, b, preferred_element_type=F32)


def _dot_nt(a, b):
    return lax.dot_general(a, b, (((1,), (1,)), ((), ())), preferred_element_type=F32)


def _dot_tn(a, b):
    return lax.dot_general(a, b, (((0,), (0,)), ((), ())), preferred_element_type=F32)


def _layer_norm(v, g, b):
    mu = jnp.mean(v, axis=-1, keepdims=True)
    c = v - mu
    var = jnp.mean(c * c, axis=-1, keepdims=True)
    return c * lax.rsqrt(var + LN_EPS) * g + b


def _sigmoid(v):
    return 0.5 * jnp.tanh(0.5 * v) + 0.5


def _resident(shape):
    nd = len(shape)
    return pl.BlockSpec(shape, lambda *_: (0,) * nd, pipeline_mode=pl.Buffered(1))


def _cparams(n_grid):
    return pltpu.CompilerParams(dimension_semantics=("arbitrary",) * n_grid,
                                vmem_limit_bytes=VMEM_LIMIT_BYTES)


def _mod_kernel(c_ref, w_ref, b_ref, o_ref):
    c = c_ref[...]
    a = c * _sigmoid(c)
    a_hi = a.astype(BF16)
    a_lo = (a - a_hi.astype(F32)).astype(BF16)
    w = w_ref[...]
    w_hi = w.astype(BF16)
    w_lo = (w - w_hi.astype(F32)).astype(BF16)
    o_ref[...] = _dot(a_hi, w_hi) + _dot(a_hi, w_lo) + _dot(a_lo, w_hi) + b_ref[...]


def _mod_call(c, ada_w, ada_b):
    bsz = c.shape[0]
    n_out = ada_w.shape[1]
    tn = D_MODEL
    return pl.pallas_call(
        _mod_kernel,
        grid=(n_out // tn,),
        in_specs=[pl.BlockSpec((bsz, D_MODEL), lambda j: (0, 0)),
                  pl.BlockSpec((D_MODEL, tn), lambda j: (0, j)),
                  pl.BlockSpec((1, tn), lambda j: (0, j))],
        out_specs=pl.BlockSpec((bsz, tn), lambda j: (0, j)),
        out_shape=jax.ShapeDtypeStruct((bsz, n_out), F32),
        compiler_params=_cparams(1),
        name="mod",
    )(c, ada_w, ada_b.reshape(1, n_out))


def _ffn_sublayer(x, sh, sc, gt, wg_ref, wu_ref, wd_ref, ln_g, ln_b):
    h = (x * (1.0 + sc) + sh).astype(BF16)
    acc = jnp.zeros(x.shape, F32)
    for start, size in FF_CHUNKS:
        a = _dot(h, wg_ref[:, start:start + size])
        up = _dot(h, wu_ref[:, start:start + size])
        act = (a * _sigmoid(a) * up).astype(BF16)
        acc = acc + _dot(act, wd_ref[start:start + size, :])
    return _layer_norm(ALPHA * x + (0.5 * (1.0 + gt)) * acc, ln_g, ln_b)


def _ffn1_kernel(x_ref, mod_ref, wg_ref, wu_ref, wd_ref, lng_ref, lnb_ref, o_ref):
    m = mod_ref[...]
    o_ref[...] = _ffn_sublayer(x_ref[...], m[0:1], m[1:2], m[2:3], wg_ref, wu_ref, wd_ref,
                               lng_ref[...], lnb_ref[...])


def _ffn1_call(x, mod, wg, wu, wd, ln_g, ln_b):
    bsz, seq, d = x.shape
    tok = pl.BlockSpec((None, TM_FFN, d), lambda b, t: (b, t, 0))
    return pl.pallas_call(
        _ffn1_kernel,
        grid=(bsz, seq // TM_FFN),
        in_specs=[tok,
                  pl.BlockSpec((None, N_MOD, d), lambda b, t: (b, 0, 0)),
                  _resident((d, D_FF)), _resident((d, D_FF)), _resident((D_FF, d)),
                  _resident((1, d)), _resident((1, d))],
        out_specs=tok,
        out_shape=jax.ShapeDtypeStruct(x.shape, F32),
        compiler_params=_cparams(2),
        name="ffn1",
    )(x, mod, wg, wu, wd, ln_g, ln_b)


def _ffn2_kernel(x_ref, yr_ref, yat_ref, mod_ref, wor_ref, woa_ref, nag_ref, ln2g_ref, ln2b_ref,
                 wg_ref, wu_ref, wd_ref, ln3g_ref, ln3b_ref, o_ref):
    m = mod_ref[...]
    x1 = x_ref[...]
    parts = []
    for t2 in range(TM_FFN // MOBA_BLOCK):
        yt = yat_ref[:, t2].reshape(D_ATTN, MOBA_BLOCK).astype(F32)
        ms = jnp.mean(yt * yt, axis=0, keepdims=True)
        yn = (yt * lax.rsqrt(ms + RMS_EPS) * nag_ref[...]).astype(BF16)
        parts.append(_dot_tn(yn, woa_ref[...]))
    y = _dot(yr_ref[...], wor_ref[...]) + jnp.concatenate(parts, axis=0)
    x2 = _layer_norm(ALPHA * x1 + (1.0 + m[5:6]) * y, ln2g_ref[...], ln2b_ref[...])
    o_ref[...] = _ffn_sublayer(x2, m[6:7], m[7:8], m[8:9], wg_ref, wu_ref, wd_ref,
                               ln3g_ref[...], ln3b_ref[...])


def _ffn2_call(x1, yr, yat, mod, wo_r, wo_a, na_g, ln2_g, ln2_b, wg, wu, wd, ln3_g, ln3_b):
    bsz, seq, d = x1.shape
    nt = TM_FFN // MOBA_BLOCK
    tok = pl.BlockSpec((None, TM_FFN, d), lambda b, t: (b, t, 0))
    return pl.pallas_call(
        _ffn2_kernel,
        grid=(bsz, seq // TM_FFN),
        in_specs=[tok,
                  pl.BlockSpec((None, TM_FFN, D_RNN), lambda b, t: (b, t, 0)),
                  pl.BlockSpec((None, N_PAIRS, nt, LANES, MOBA_BLOCK), lambda b, t: (b, 0, t, 0, 0)),
                  pl.BlockSpec((None, N_MOD, d), lambda b, t: (b, 0, 0)),
                  _resident((D_RNN, d)), _resident((D_ATTN, d)), _resident((D_ATTN, 1)),
                  _resident((1, d)), _resident((1, d)),
                  _resident((d, D_FF)), _resident((d, D_FF)), _resident((D_FF, d)),
                  _resident((1, d)), _resident((1, d))],
        out_specs=tok,
        out_shape=jax.ShapeDtypeStruct(x1.shape, F32),
        compiler_params=_cparams(2),
        name="ffn2",
    )(x1, yr, yat, mod, wo_r, wo_a, na_g, ln2_g, ln2_b, wg, wu, wd, ln3_g, ln3_b)


def _inproj_kernel(x_ref, mod_ref, posr_ref, invc_ref, expand_ref, wugk_ref, wqvt_ref,
                   cw_ref, cb_ref, wgate_ref, bgate_ref, lam_ref, ng_ref,
                   yr_ref, k_ref, km_ref, qt_ref, vt_ref, tail_ref, hstate_ref):
    @pl.when(pl.program_id(1) == 0)
    def _():
        tail_ref[...] = jnp.zeros(tail_ref.shape, F32)
        hstate_ref[...] = jnp.zeros(hstate_ref.shape, F32)

    m = mod_ref[...]
    h = (x_ref[...] * (1.0 + m[4:5]) + m[3:4]).astype(BF16)
    ugk = _dot(h, wugk_ref[...])
    yr_ref[...] = _rglru_tile(ugk[:, :D_RNN], ugk[:, D_RNN:2 * D_RNN], cw_ref, cb_ref, wgate_ref, bgate_ref,
                              lam_ref, ng_ref, tail_ref, hstate_ref).astype(yr_ref.dtype)

    ang_t = invc_ref[...] * posr_ref[...].astype(F32)
    cos_t = jnp.cos(ang_t)
    sin_t = jnp.sin(ang_t)

    def bf16_terms(v):
        hi = v.astype(BF16).astype(F32)
        mid = (v - hi).astype(BF16).astype(F32)
        return [hi, mid, v - hi - mid]

    terms = bf16_terms(cos_t) + bf16_terms(sin_t) + [jnp.ones_like(cos_t), jnp.zeros_like(cos_t)]
    trig = _dot_tn(jnp.concatenate(terms, axis=0).astype(BF16), expand_ref[...])
    cos_l = trig[:, :LANES]
    sin_l = trig[:, LANES:]

    lane = lax.broadcasted_iota(jnp.int32, (1, LANES), 1) % HEAD_DIM
    s_lo = jnp.where(lane < ROPE_HALF, -sin_l, 0.0)
    s_hi = jnp.where((lane >= ROPE_HALF) & (lane < ROPE_DIM), sin_l, 0.0)
    kcols = []
    for c in range(D_ATTN // LANES):
        kc = ugk[:, 2 * D_RNN + c * LANES: 2 * D_RNN + (c + 1) * LANES]
        kcols.append(kc * cos_l + pltpu.roll(kc, LANES - ROPE_HALF, 1) * s_lo
                     + pltpu.roll(kc, ROPE_HALF, 1) * s_hi)
    k = jnp.concatenate(kcols, axis=1)
    nblk = TM_PROJ // MOBA_BLOCK
    lane_oh = lax.broadcasted_iota(jnp.int32, (MOBA_BLOCK, LANES), 1)
    for t2 in range(nblk):
        rows_t2 = slice(t2 * MOBA_BLOCK, (t2 + 1) * MOBA_BLOCK)
        onehot = (lane_oh == pl.program_id(1) * nblk + t2).astype(BF16)
        pieces = []
        for c in range(N_PAIRS):
            pieces += [kcols[c][rows_t2].astype(BF16), onehot]
        k_ref[rows_t2, :] = jnp.concatenate(pieces, axis=1)
    km_ref[...] = jnp.concatenate(
        [jnp.mean(k[t2 * MOBA_BLOCK:(t2 + 1) * MOBA_BLOCK], axis=0, keepdims=True) for t2 in range(nblk)],
        axis=0)

    qvt = _dot_nt(wqvt_ref[...], h)
    scale = math.log2(math.e) / math.sqrt(HEAD_DIM)
    rows = []
    for hd in range(N_HEADS):
        base = hd * HEAD_DIM
        x1 = qvt[base:base + ROPE_HALF]
        x2 = qvt[base + ROPE_HALF:base + ROPE_DIM]
        rows += [x1 * cos_t - x2 * sin_t, x2 * cos_t + x1 * sin_t, qvt[base + ROPE_DIM:base + HEAD_DIM]]
    qt = (jnp.concatenate(rows, axis=0) * scale).astype(BF16)
    vt = qvt[D_ATTN:].astype(BF16)
    for t2 in range(nblk):
        sl = slice(t2 * MOBA_BLOCK, (t2 + 1) * MOBA_BLOCK)
        qt_ref[:, t2] = qt[:, sl].reshape(N_PAIRS, LANES, MOBA_BLOCK)
        vt_ref[:, t2] = vt[:, sl].reshape(N_PAIRS, LANES, MOBA_BLOCK)


def _inproj_call(x1, mod, positions, inv_freq, w_ugk, w_qvt, conv_w, conv_b, w_gate, b_gate, lam, norm_g):
    bsz, seq, d = x1.shape
    nt = TM_PROJ // MOBA_BLOCK
    nblk = seq // MOBA_BLOCK
    lane = jnp.arange(LANES) % HEAD_DIM
    rotary_lane = lane < ROPE_DIM
    route = ((jnp.arange(ROPE_HALF)[:, None] == (lane % ROPE_HALF)[None, :]) & rotary_lane[None, :]).astype(F32)
    none = jnp.zeros_like(route)
    ones_row = jnp.zeros_like(route).at[0].set(jnp.where(rotary_lane, 0.0, 1.0))
    expand = jnp.concatenate(
        [jnp.concatenate([route, none], axis=1)] * 3 + [jnp.concatenate([none, route], axis=1)] * 3
        + [jnp.concatenate([ones_row, none], axis=1), jnp.concatenate([none, none], axis=1)], axis=0).astype(BF16)
    pos_row = positions.reshape(bsz, seq // TM_PROJ, 1, TM_PROJ)
    tok512 = pl.BlockSpec((None, TM_PROJ, D_RNN), lambda b, t: (b, t, 0))
    blk5 = pl.BlockSpec((None, N_PAIRS, nt, LANES, MOBA_BLOCK), lambda b, t: (b, 0, t, 0, 0))
    shape5 = (bsz, N_PAIRS, nblk, LANES, MOBA_BLOCK)
    return pl.pallas_call(
        _inproj_kernel,
        grid=(bsz, seq // TM_PROJ),
        in_specs=[pl.BlockSpec((None, TM_PROJ, d), lambda b, t: (b, t, 0)),
                  pl.BlockSpec((None, N_MOD, d), lambda b, t: (b, 0, 0)),
                  pl.BlockSpec((None, None, 1, TM_PROJ), lambda b, t: (b, t, 0, 0)),
                  _resident((ROPE_HALF, 1)), _resident((8 * ROPE_HALF, 2 * LANES)),
                  _resident((d, 3 * D_RNN)), _resident((2 * D_ATTN, d)),
                  _resident((CONV_WIDTH, D_RNN)), _resident((1, D_RNN)),
                  _resident((D_RNN, 2 * D_RNN)), _resident((1, 2 * D_RNN)),
                  _resident((1, D_RNN)), _resident((1, D_RNN))],
        out_specs=[tok512,
                   pl.BlockSpec((None, TM_PROJ, N_PAIRS * K_AUG), lambda b, t: (b, t, 0)),
                   pl.BlockSpec((None, None, nt, D_ATTN), lambda b, t: (b, t, 0, 0)),
                   blk5, blk5],
        out_shape=[jax.ShapeDtypeStruct((bsz, seq, D_RNN), BF16),
                   jax.ShapeDtypeStruct((bsz, seq, N_PAIRS * K_AUG), BF16),
                   jax.ShapeDtypeStruct((bsz, seq // TM_PROJ, nt, D_ATTN), F32),
                   jax.ShapeDtypeStruct(shape5, BF16),
                   jax.ShapeDtypeStruct(shape5, BF16)],
        scratch_shapes=[pltpu.VMEM((8, D_RNN), F32), pltpu.VMEM((8, D_RNN), F32)],
        compiler_params=_cparams(2),
        name="inproj",
    )(x1, mod, pos_row, inv_freq.reshape(ROPE_HALF, 1), expand, w_ugk, w_qvt,
      conv_w, conv_b, w_gate, b_gate, lam, norm_g)


def _rglru_tile(u, g, cw_ref, cb_ref, wgate_ref, bgate_ref, lam_ref, ng_ref, tail_ref, h_ref):
    tt = u.shape[0]
    sub = 8
    ext = jnp.concatenate([tail_ref[...], u], axis=0)
    tail_ref[...] = u[tt - sub:tt]
    cw = cw_ref[...]
    uc = u * cw[3:4] + cb_ref[...]
    for back in range(1, CONV_WIDTH):
        uc = uc + pltpu.roll(ext, back, 0)[sub:] * cw[3 - back:4 - back]

    gates = _dot(uc.astype(BF16), wgate_ref[...]) + bgate_ref[...]
    r = _sigmoid(gates[:, :D_RNN])
    ig = _sigmoid(gates[:, D_RNN:])
    z = -lam_ref[...]
    softplus = jnp.maximum(z, 0.0) + jnp.log1p(jnp.exp(-jnp.abs(z)))
    log_a = (-LRU_C) * r * softplus
    a = jnp.exp(log_a)
    one_m_a2 = 1.0 - a * a
    b = (one_m_a2 * lax.rsqrt(jnp.maximum(one_m_a2, 1e-30))) * (ig * uc)

    a = a.reshape(tt // sub, sub, D_RNN)
    b = b.reshape(tt // sub, sub, D_RNN)
    row_in_grp = lax.broadcasted_iota(jnp.int32, (1, sub, 1), 1)
    d = 1
    while d < sub:
        keep = row_in_grp >= d
        a_prev = jnp.where(keep, pltpu.roll(a, d, 1), 1.0)
        b_prev = jnp.where(keep, pltpu.roll(b, d, 1), 0.0)
        b = a * b_prev + b
        a = a * a_prev
        d *= 2
    a = a.reshape(tt, D_RNN)
    b = b.reshape(tt, D_RNN)
    carry = h_ref[0:1, :]
    groups = []
    for grp in range(tt // sub):
        hg = a[grp * sub:(grp + 1) * sub] * carry + b[grp * sub:(grp + 1) * sub]
        groups.append(hg)
        carry = hg[sub - 1:sub]
    h = jnp.concatenate(groups, axis=0)
    h_ref[...] = jnp.broadcast_to(carry, h_ref.shape)

    gelu = 0.5 * g * (1.0 + jnp.tanh(math.sqrt(2.0 / math.pi) * (g + 0.044715 * (g * g * g))))
    y = h * gelu
    ms = jnp.mean(y * y, axis=-1, keepdims=True)
    return y * lax.rsqrt(ms + RMS_EPS) * ng_ref[...]


def _top3_bias(gate, qblk):
    nblk = gate.shape[0]
    jidx = lax.broadcasted_iota(jnp.int32, gate.shape, 0).astype(F32)
    own = qblk.astype(F32)
    valid = jidx < own
    g = jnp.where(valid, gate, NEG_INF)
    sel = jnp.zeros(gate.shape, jnp.bool_)
    for _ in range(MOBA_TOPK):
        mx = jnp.max(g, axis=0, keepdims=True)
        first = jnp.min(jnp.where(g == mx, jidx, float(nblk)), axis=0, keepdims=True)
        pick = jidx == first
        sel = sel | pick
        g = jnp.where(pick, -jnp.inf, g)
    return jnp.where((sel & valid) | (jidx == own), 0.0, NEG_INF)


def _moba_kernel(qt_ref, k_ref, vt_ref, km_ref, o_ref, s0_ref, s1_ref, mx0_ref, mx1_ref, m_ref, acc_ref):
    npr, ntl = qt_ref.shape[:2]
    i0 = pl.program_id(2) * ntl
    nblk = km_ref.shape[0]
    ckeys = MOBA_CHUNK * MOBA_BLOCK
    acc_rows = acc_ref.shape[1]
    streams = [(pr, tl, hd) for pr in range(npr) for tl in range(ntl) for hd in range(2)]

    q_aug = []
    for pr in range(npr):
        km = km_ref[:, pr * LANES:(pr + 1) * LANES]
        lane_km = lax.broadcasted_iota(jnp.int32, km.shape, 1)
        km2 = jnp.concatenate([jnp.where(lane_km < HEAD_DIM, km, 0.0),
                               jnp.where(lane_km >= HEAD_DIM, km, 0.0)], axis=0).astype(BF16)
        for tl in range(ntl):
            qt = qt_ref[pr, tl]
            gate = _dot(km2, qt)
            feat = lax.broadcasted_iota(jnp.int32, qt.shape, 0)
            for hd in range(2):
                q_head = jnp.where((feat >= HEAD_DIM) if hd else (feat < HEAD_DIM), qt, jnp.zeros_like(qt))
                bias = _top3_bias(gate[hd * nblk:(hd + 1) * nblk], i0 + tl).astype(BF16)
                q_aug.append(jnp.concatenate(
                    [q_head, bias, jnp.zeros((K_AUG - LANES - nblk, MOBA_BLOCK), BF16)], axis=0))

    key_in_blk = lax.broadcasted_iota(jnp.int32, (MOBA_BLOCK, MOBA_BLOCK), 0)
    qry_in_blk = lax.broadcasted_iota(jnp.int32, (MOBA_BLOCK, MOBA_BLOCK), 1)
    ones_rows = jnp.ones((acc_rows - HEAD_DIM, MOBA_BLOCK), BF16)

    def scores(c, dst, group, nblocks=MOBA_CHUNK):
        s_dst, mx_dst = dst
        nkeys = nblocks * MOBA_BLOCK
        rows = pl.ds(pl.multiple_of(c * ckeys, ckeys), nkeys)
        for n in group:
            pr = streams[n][0]
            s = _dot(k_ref[rows, pr * K_AUG:(pr + 1) * K_AUG], q_aug[n])
            s_dst[n, 0:nkeys, :] = s
            mx_dst[n] = jnp.max(s.reshape(nkeys // 8, 8, MOBA_BLOCK), axis=0)

    def softmax_pv(c, src, own_blocks, group):
        s_src, mx_src = src
        for n in group:
            pr, tl, hd = streams[n]
            m_old = m_ref[n]
            own = None if own_blocks is None else own_blocks - ntl + tl
            visited = MOBA_CHUNK if own is None else own + 1

            def piece(r):
                s = s_src[n, r * MOBA_BLOCK:(r + 1) * MOBA_BLOCK, :]
                if r == own:
                    s = jnp.where(key_in_blk <= qry_in_blk, s, NEG_INF)
                return s

            if own is None:
                mx = mx_src[n]
            else:
                mx = piece(0)
                for r in range(1, visited):
                    mx = jnp.maximum(mx, piece(r))
            m_new = jnp.maximum(m_old, jnp.max(mx, axis=0, keepdims=True))
            acc = jnp.exp2(m_old - m_new) * acc_ref[n]
            for r in range(visited):
                p = jnp.exp2(piece(r) - m_new).astype(BF16)
                vt_aug = jnp.concatenate(
                    [vt_ref[pr, c * MOBA_CHUNK + r, hd * HEAD_DIM:(hd + 1) * HEAD_DIM, :], ones_rows], axis=0)
                acc = acc + _dot(vt_aug, p)
            m_ref[n] = m_new
            acc_ref[n] = acc

    def step(c, s_cur, s_nxt, group, next_blocks=MOBA_CHUNK):
        scores(c + 1, s_nxt, group[:1], next_blocks)
        for idx, n in enumerate(group):
            if idx + 1 < len(group):
                scores(c + 1, s_nxt, group[idx + 1:idx + 2], next_blocks)
            softmax_pv(c, s_cur, None, [n])

    m_ref[...] = jnp.full(m_ref.shape, NEG_INF, F32)
    acc_ref[...] = jnp.zeros(acc_ref.shape, F32)
    last = lax.shift_right_logical(i0, MOBA_CHUNK_LOG2)
    buf0, buf1 = (s0_ref, mx0_ref), (s1_ref, mx1_ref)
    group = list(range(len(streams)))
    scores(0, buf0, group)

    def two_steps(u, _):
        step(2 * u, buf0, buf1, group)
        step(2 * u + 1, buf1, buf0, group)
        return 0

    lax.fori_loop(0, lax.shift_right_logical(last, 1), two_steps, 0)

    used = (i0 & (MOBA_CHUNK - 1)) + ntl
    for nblocks in range(ntl, MOBA_CHUNK + 1, ntl):
        @pl.when(((last & 1) == 1) & (used == nblocks))
        def _(nblocks=nblocks):
            step(last - 1, buf0, buf1, group, nblocks)
            softmax_pv(last, buf1, nblocks, group)

        @pl.when(((last & 1) == 0) & (used == nblocks))
        def _(nblocks=nblocks):
            softmax_pv(last, buf0, nblocks, group)

    for n in range(0, len(streams), 2):
        pr, tl, _ = streams[n]
        o_ref[pr, tl] = jnp.concatenate(
            [acc_ref[n + hd, :HEAD_DIM, :] / acc_ref[n + hd, HEAD_DIM:HEAD_DIM + 1, :] for hd in range(2)],
            axis=0).astype(o_ref.dtype)


def _moba_call(qt, k, vt, kmean):
    bsz, npair, nblk, _, _ = qt.shape
    seq = k.shape[1]
    npr, ntl = MOBA_PAIRS_PER_STEP, MOBA_QBLOCKS_PER_STEP
    n_streams = 2 * npr * ntl
    qblk = pl.BlockSpec((None, npr, ntl, LANES, MOBA_BLOCK), lambda b, p, i: (b, p, i, 0, 0))
    return pl.pallas_call(
        _moba_kernel,
        grid=(bsz, npair // npr, nblk // ntl),
        in_specs=[qblk,
                  pl.BlockSpec((None, seq, npr * K_AUG), lambda b, p, i: (b, 0, p)),
                  pl.BlockSpec((None, npr, nblk, LANES, MOBA_BLOCK), lambda b, p, i: (b, p, 0, 0, 0)),
                  pl.BlockSpec((None, nblk, npr * LANES), lambda b, p, i: (b, 0, p))],
        out_specs=qblk,
        out_shape=jax.ShapeDtypeStruct(qt.shape, BF16),
        scratch_shapes=[pltpu.VMEM((n_streams, MOBA_CHUNK * MOBA_BLOCK, MOBA_BLOCK), F32),
                        pltpu.VMEM((n_streams, MOBA_CHUNK * MOBA_BLOCK, MOBA_BLOCK), F32),
                        pltpu.VMEM((n_streams, 8, MOBA_BLOCK), F32),
                        pltpu.VMEM((n_streams, 8, MOBA_BLOCK), F32),
                        pltpu.VMEM((n_streams, 1, MOBA_BLOCK), F32),
                        pltpu.VMEM((n_streams, HEAD_DIM + 16, MOBA_BLOCK), F32)],
        compiler_params=_cparams(3),
        name="moba",
    )(qt, k, vt, kmean)


def _block_diag(w):
    n, d, e = w.shape
    eye = jnp.eye(n, dtype=w.dtype)
    return (eye[:, None, :, None] * w[:, :, None, :]).reshape(n * d, n * e)


def kernel(x, c, positions, ada_w, ada_b, ffn1_w_gate, ffn1_w_up, ffn1_w_down, ln1_g, ln1_b, w_in, conv_w, conv_b, lru_wa, lru_ba, lru_wx, lru_bx, lru_lambda, norm_rnn_g, norm_attn_g, w_out, ln2_g, ln2_b, ffn2_w_gate, ffn2_w_up, ffn2_w_down, ln3_g, ln3_b):
    bsz, seq, d = x.shape
    depth = ada_w.shape[0]
    assert d == D_MODEL and w_in.shape[-1] == 2 * D_RNN + 3 * D_ATTN and ffn1_w_gate.shape[-1] == D_FF
    assert seq % (MOBA_BLOCK * MOBA_CHUNK) == 0 and seq % TM_FFN == 0 and seq % TM_PROJ == 0
    assert seq // MOBA_BLOCK <= LANES, "one-hot key-block columns must fit one lane group"
    inv_freq =ROPE_THETA ** (-jnp.arange(ROPE_HALF, dtype=F32) / ROPE_HALF)
    row = lambda v: v.reshape(1, -1)
    for l in range(depth):
        mod = _mod_call(c, ada_w[l], ada_b[l]).reshape(bsz, N_MOD, d)
        x1 = _ffn1_call(x, mod, ffn1_w_gate[l].astype(BF16), ffn1_w_up[l].astype(BF16),
                        ffn1_w_down[l].astype(BF16), row(ln1_g[l]), row(ln1_b[l]))

        w_ugk = jnp.concatenate([w_in[l][:, :2 * D_RNN], w_in[l][:, 2 * D_RNN + D_ATTN:2 * D_RNN + 2 * D_ATTN]],
                                axis=1).astype(BF16)
        w_qvt = jnp.concatenate([w_in[l][:, 2 * D_RNN:2 * D_RNN + D_ATTN], w_in[l][:, 2 * D_RNN + 2 * D_ATTN:]],
                                axis=1).T.astype(BF16)
        w_gate = jnp.concatenate([_block_diag(lru_wa[l]), _block_diag(lru_wx[l])], axis=1).astype(BF16)
        b_gate = jnp.concatenate([lru_ba[l], lru_bx[l]]).reshape(1, -1)
        yr, k, kmean, qt, vt = _inproj_call(x1, mod, positions, inv_freq, w_ugk, w_qvt, conv_w[l], row(conv_b[l]),
                                            w_gate, b_gate, row(lru_lambda[l]), row(norm_rnn_g[l]))

        yat = _moba_call(qt, k, vt, kmean.reshape(bsz, seq // MOBA_BLOCK, D_ATTN))

        x = _ffn2_call(x1, yr, yat, mod, w_out[l][:D_RNN].astype(BF16), w_out[l][D_RNN:].astype(BF16),
                       norm_attn_g[l].reshape(D_ATTN, 1), row(ln2_g[l]), row(ln2_b[l]),
                       ffn2_w_gate[l].astype(BF16), ffn2_w_up[l].astype(BF16), ffn2_w_down[l].astype(BF16),
                       row(ln3_g[l]), row(ln3_b[l]))
    return x
```
